```python
import math
import jax, jax.numpy as jnp
from jax import lax
import numpy as np

D_MODEL = 2048
BATCH = 4
SEQ = 2048
DEPTH = 4
DEC_BATCH = 4
DEC_SEQ = 4096
PAST_LEN = 128

POOL_WIDTH = D_MODEL // 2
POOL_WINDOWS = (2, 4, 8, 16)
POOL_GROUP = POOL_WIDTH // len(POOL_WINDOWS)
HEAD_DIM = 128
ATTN_GROUPS = ((128, 1), (512, 4), (2048, 16))
HEADS_PER_GROUP = 4
N_HEADS = HEADS_PER_GROUP * len(ATTN_GROUPS)
ATTN_WIDTH = N_HEADS * HEAD_DIM
BAND_BLOCK = 64
ROPE_THETA = 10000.0
NORM_EPS = 1e-6
NEG_BIG = -1e30
SPLIT_SIZES = (POOL_WIDTH, POOL_WIDTH, ATTN_WIDTH, ATTN_WIDTH, ATTN_WIDTH, ATTN_WIDTH, D_MODEL, D_MODEL)
IN_WIDTH = sum(SPLIT_SIZES)

kernel_name = "gated_pool_dilated_attn_encoder"


def rmsnorm(x, gain):
    xf = x.astype(jnp.float32)
    xf = xf * lax.rsqrt(jnp.mean(xf * xf, axis=-1, keepdims=True) + NORM_EPS)
    return (xf * gain.astype(jnp.float32)).astype(x.dtype)


def rope(x, pos):
    half = HEAD_DIM // 2
    inv = ROPE_THETA ** (-jnp.arange(half, dtype=jnp.float32) / half)
    ang = pos[:, None] * inv[None, :]
    cos = jnp.cos(ang)[None, :, None, :]
    sin = jnp.sin(ang)[None, :, None, :]
    xf = x.astype(jnp.float32)
    x1, x2 = xf[..., :half], xf[..., half:]
    out = jnp.concatenate([x1 * cos - x2 * sin, x2 * cos + x1 * sin], axis=-1)
    return out.astype(x.dtype)


def multiscale_pool(h, w_grp, scale):
    B, S, C = h.shape
    hf = h.astype(jnp.float32)
    cs = jnp.concatenate([jnp.zeros((B, 1, C), jnp.float32), jnp.cumsum(hf, axis=1)], axis=1)
    t = jnp.arange(S)
    outs = []
    for g, w in enumerate(POOL_WINDOWS):
        sl = slice(g * POOL_GROUP, (g + 1) * POOL_GROUP)
        cs_g = cs[..., sl]
        lo = jnp.clip(t - w // 2, 0, S)
        hi = jnp.clip(t + w // 2, 0, S)
        cnt = (hi - lo).astype(jnp.float32)[None, :, None]
        mean = (cs_g[:, hi] - cs_g[:, lo]) / cnt
        outs.append(mean - hf[..., sl])
    p = jnp.stack(outs, axis=2).astype(h.dtype)
    y = jnp.einsum('bsgc,gcd->bsgd', p, w_grp).reshape(B, S, C)
    return y * scale


def dilated_band_attention(q, k, v, dil, radius):
    B, S, H, Dh = q.shape
    L = S // dil
    Lp = -(-L // BAND_BLOCK) * BAND_BLOCK
    nb = Lp // BAND_BLOCK

    def to_cls(t):
        return t.reshape(B, L, dil, H, Dh).transpose(0, 2, 3, 1, 4)

    qc, kc, vc = to_cls(q), to_cls(k), to_cls(v)
    qb = jnp.pad(qc, ((0, 0), (0, 0), (0, 0), (0, Lp - L), (0, 0))).reshape(B, dil, H, nb, BAND_BLOCK, Dh)

    def band(t):
        tp = jnp.pad(t, ((0, 0), (0, 0), (0, 0), (BAND_BLOCK, Lp - L + BAND_BLOCK), (0, 0)))
        tb = tp.reshape(B, dil, H, nb + 2, BAND_BLOCK, Dh)
        return jnp.concatenate([tb[:, :, :, :-2], tb[:, :, :, 1:-1], tb[:, :, :, 2:]], axis=4)

    kb, vb = band(kc), band(vc)
    blk = jnp.arange(nb)[:, None]
    qi = blk * BAND_BLOCK + jnp.arange(BAND_BLOCK)[None, :]
    kj = (blk - 1) * BAND_BLOCK + jnp.arange(3 * BAND_BLOCK)[None, :]
    diff = kj[:, None, :] - qi[:, :, None]
    mask = (jnp.abs(diff) <= radius) & (kj[:, None, :] >= 0) & (kj[:, None, :] < L)

    s = jnp.einsum('bdhnqc,bdhnkc->bdhnqk', qb, kb, preferred_element_type=jnp.float32) / math.sqrt(Dh)
    s = jnp.where(mask, s, NEG_BIG)
    m = jnp.max(s, axis=-1, keepdims=True)
    p = jnp.exp(s - m)
    l = jnp.sum(p, axis=-1, keepdims=True)
    o = jnp.einsum('bdhnqk,bdhnkc->bdhnqc', p, vb.astype(jnp.float32)) / l
    lse = (m + jnp.log(l))[..., 0]
    o = o.reshape(B, dil, H, Lp, Dh)[:, :, :, :L].transpose(0, 3, 1, 2, 4).reshape(B, S, H, Dh)
    lse = lse.reshape(B, dil, H, Lp)[:, :, :, :L].transpose(0, 3, 1, 2).reshape(B, S, H)
    return o.astype(q.dtype), lse


def dilated_mixture_attention(q, k, v):
    B, S, _ = q.shape
    pos = jnp.arange(S, dtype=jnp.float32)
    q = rope(q.reshape(B, S, N_HEADS, HEAD_DIM), pos)
    k = rope(k.reshape(B, S, N_HEADS, HEAD_DIM), pos)
    v = v.reshape(B, S, N_HEADS, HEAD_DIM)
    outs, lses = [], []
    for g, (win, dil) in enumerate(ATTN_GROUPS):
        sl = slice(g * HEADS_PER_GROUP, (g + 1) * HEADS_PER_GROUP)
        o, lse = dilated_band_attention(q[:, :, sl], k[:, :, sl], v[:, :, sl], dil, win // (2 * dil))
        outs.append(o)
        lses.append(lse)
    wts = jax.nn.softmax(jnp.stack(lses, axis=0), axis=0)
    o = jnp.concatenate([outs[g] * wts[g][..., None].astype(q.dtype) for g in range(len(ATTN_GROUPS))], axis=2)
    return o.reshape(B, S, ATTN_WIDTH)


def encoder_layer(x, c, norm_gain, w_ada, b_ada, w_in, w_pool_grp, pool_scale, w_proj_pool, w_proj_attn, w_out):
    mod = c @ w_ada + b_ada
    shift, scale, gate = jnp.split(mod, 3, axis=-1)
    h = rmsnorm(x, norm_gain) * (1 + scale[:, None, :]) + shift[:, None, :]
    z = h @ w_in
    idx = list(np.cumsum(SPLIT_SIZES)[:-1])
    pool_in, pool_gate, q, k, v, attn_gate, g_pool, g_attn = jnp.split(z, idx, axis=-1)
    a = multiscale_pool(pool_in, w_pool_grp, pool_scale) * jax.nn.silu(pool_gate)
    b = dilated_mixture_attention(q, k, v) * jax.nn.silu(attn_gate)
    merged = jax.nn.sigmoid(g_pool) * (a @ w_proj_pool) + jax.nn.sigmoid(g_attn) * (b @ w_proj_attn)
    return x + gate[:, None, :] * (merged @ w_out)


def run_trunk(x, c, norm_gain, w_ada, b_ada, w_in, w_pool_grp, pool_scale, w_proj_pool, w_proj_attn, w_out, final_gain):
    for i in range(DEPTH):
        x = encoder_layer(x, c, norm_gain[i], w_ada[i], b_ada[i], w_in[i], w_pool_grp[i], pool_scale[i],
                          w_proj_pool[i], w_proj_attn[i], w_out[i])
    return rmsnorm(x, final_gain)


def setup_inputs(seed: int = 0) -> dict:
    key = jax.random.key(seed)
    ks = jax.random.split(key, 14)
    f32 = jnp.float32
    nrm = lambda k, shape: jax.random.normal(k, shape, f32)
    return {
        "x_prompt": nrm(ks[0], (BATCH, SEQ, D_MODEL)),
        "x_sample": nrm(ks[1], (DEC_BATCH, DEC_SEQ, D_MODEL)),
        "c_prompt": nrm(ks[2], (BATCH, D_MODEL)),
        "c_sample": nrm(ks[3], (DEC_BATCH, D_MODEL)),
        "norm_gain": 1.0 + 0.02 * nrm(ks[4], (DEPTH, D_MODEL)),
        "w_ada": nrm(ks[5], (DEPTH, D_MODEL, 3 * D_MODEL)) * (0.2 * D_MODEL ** -0.5),
        "b_ada": 0.02 * nrm(ks[6], (DEPTH, 3 * D_MODEL)),
        "w_in": nrm(ks[7], (DEPTH, D_MODEL, IN_WIDTH)) * D_MODEL ** -0.5,
        "w_pool_grp": nrm(ks[8], (DEPTH, len(POOL_WINDOWS), POOL_GROUP, POOL_GROUP)) * POOL_GROUP ** -0.5,
        "pool_scale": 1.0 + 0.1 * nrm(ks[9], (DEPTH, POOL_WIDTH)),
        "w_proj_pool": nrm(ks[10], (DEPTH, POOL_WIDTH, D_MODEL)) * POOL_WIDTH ** -0.5,
        "w_proj_attn": nrm(ks[11], (DEPTH, ATTN_WIDTH, D_MODEL)) * ATTN_WIDTH ** -0.5,
        "w_out": nrm(ks[12], (DEPTH, D_MODEL, D_MODEL)) * D_MODEL ** -0.5,
        "final_gain": 1.0 + 0.02 * nrm(ks[13], (D_MODEL,)),
    }


def reference(x_prompt, x_sample, c_prompt, c_sample, norm_gain, w_ada, b_ada, w_in, w_pool_grp, pool_scale,
              w_proj_pool, w_proj_attn, w_out, final_gain):
    y_prompt = run_trunk(x_prompt, c_prompt, norm_gain, w_ada, b_ada, w_in, w_pool_grp, pool_scale,
                         w_proj_pool, w_proj_attn, w_out, final_gain)
    y_sample = run_trunk(x_sample, c_sample, norm_gain, w_ada, b_ada, w_in, w_pool_grp, pool_scale,
                         w_proj_pool, w_proj_attn, w_out, final_gain)
    return (y_prompt, y_sample)
```

```python
import functools
import math

import jax
import jax.numpy as jnp
from jax import lax
from jax.experimental import pallas as pl
from jax.experimental.pallas import tpu as pltpu

F32 = jnp.float32
BF16 = jnp.bfloat16

D_MODEL = 2048
POOL_WIDTH = D_MODEL // 2
POOL_WINDOWS = (2, 4, 8, 16)
POOL_GROUP = POOL_WIDTH // len(POOL_WINDOWS)
POOL_HALO = max(POOL_WINDOWS) // 2
HEAD_DIM = 128
ATTN_GROUPS = ((128, 1), (512, 4), (2048, 16))
HEADS_PER_GROUP = 4
GROUP_WIDTH = HEADS_PER_GROUP * HEAD_DIM
ATTN_WIDTH = GROUP_WIDTH * len(ATTN_GROUPS)
BAND_RADIUS = 64
assert all(win // (2 * dil) == BAND_RADIUS for win, dil in ATTN_GROUPS)
ROPE_THETA = 10000.0
NORM_EPS = 1e-6
NEG_BIG = -1e30
IN_WIDTH = 2 * POOL_WIDTH + 4 * ATTN_WIDTH + 2 * D_MODEL

COL_POOL_IN = 0
COL_POOL_GATE = POOL_WIDTH
COL_Q = 2 * POOL_WIDTH
COL_K = COL_Q + ATTN_WIDTH
COL_V = COL_K + ATTN_WIDTH
COL_ATTN_GATE = COL_V + ATTN_WIDTH
COL_G_POOL = COL_ATTN_GATE + ATTN_WIDTH
COL_G_ATTN = COL_G_POOL + D_MODEL

VMEM_LIMIT_BYTES = 56 * 1024 * 1024
BF16_SUBLANES = 16
F32_SUBLANES = 8

INPROJ_TM = 1024
INPROJ_TN = 1024
ATTN_BQ = 512
ATTN_SUB = 128
MERGE_TM = 256
NORM_TM = 512


def _sigmoid(x):
    return 1.0 / (1.0 + jnp.exp(-x))


def _silu(x):
    return x * _sigmoid(x)


def _locate(row0, lay, dil=1):
    bp, sp, bs, ss = lay
    rows_p, lp, ls = bp * sp // dil, sp // dil, ss // dil
    in_p = row0 < rows_p
    seq = jnp.where(in_p, row0 // lp, bp + (row0 - rows_p) // ls)
    local = jnp.where(in_p, row0 % lp, (row0 - rows_p) % ls)
    length = jnp.where(in_p, lp, ls)
    return seq, local, length


def _mod_kernel(c_ref, w_ref, b_ref, o_ref):
    c = c_ref[...].astype(BF16)
    w = w_ref[0].astype(BF16)
    o_ref[0] = jnp.dot(c, w, preferred_element_type=F32) + b_ref[0]


def _modulation(c, w_ada, b_ada):
    depth, d, n = w_ada.shape
    nseq = c.shape[0]
    tn = 1536
    return pl.pallas_call(
        _mod_kernel,
        grid=(depth, n // tn),
        in_specs=[
            pl.BlockSpec((nseq, d), lambda l, j: (0, 0)),
            pl.BlockSpec((1, d, tn), lambda l, j: (l, 0, j)),
            pl.BlockSpec((1, 1, tn), lambda l, j: (l, 0, j)),
        ],
        out_specs=pl.BlockSpec((1, nseq, tn), lambda l, j: (l, 0, j)),
        out_shape=jax.ShapeDtypeStruct((depth, nseq, n), F32),
        compiler_params=pltpu.CompilerParams(
            dimension_semantics=("arbitrary", "arbitrary"), vmem_limit_bytes=VMEM_LIMIT_BYTES),
        name="modulation",
    )(c, w_ada, b_ada.reshape(depth, 1, n))


def _rope_table_kernel(cos_ref, sin_ref):
    s = cos_ref.shape[0]
    half = HEAD_DIM // 2
    pos = lax.broadcasted_iota(jnp.int32, (s, HEAD_DIM), 0).astype(F32)
    lane = lax.broadcasted_iota(jnp.int32, (s, HEAD_DIM), 1)
    idx = jnp.where(lane < half, lane, lane - half).astype(F32)
    inv = jnp.exp(idx * (-math.log(ROPE_THETA) / half))
    ang = pos * inv
    cos_ref[...] = jnp.cos(ang)
    sin_ref[...] = jnp.where(lane < half, -jnp.sin(ang), jnp.sin(ang))


def _rope_tables(s):
    return pl.pallas_call(
        _rope_table_kernel,
        out_shape=(jax.ShapeDtypeStruct((s, HEAD_DIM), F32), jax.ShapeDtypeStruct((s, HEAD_DIM), F32)),
        name="rope_tables",
    )()


def _inproj_kernel(x_ref, g_ref, shift_ref, scale_ref, w_ref, cos_ref, sin_ref, z_ref, h_ref, *, rope_lo, rope_hi):
    j = pl.program_id(1)

    @pl.when(j == 0)
    def _():
        x = x_ref[...]
        xn = x * lax.rsqrt(jnp.mean(x * x, axis=-1, keepdims=True) + NORM_EPS) * g_ref[...]
        h_ref[...] = (xn * (1.0 + scale_ref[0]) + shift_ref[0]).astype(BF16)

    acc = jnp.dot(h_ref[...], w_ref[...], preferred_element_type=F32)
    is_rope = (j >= rope_lo) & (j < rope_hi)

    @pl.when(is_rope)
    def _():
        cos = cos_ref[...]
        sin = sin_ref[...]
        for hh in range(acc.shape[1] // HEAD_DIM):
            cs = slice(hh * HEAD_DIM, (hh + 1) * HEAD_DIM)
            xh = acc[:, cs]
            z_ref[:, cs] = (xh * cos + pltpu.roll(xh, HEAD_DIM // 2, 1) * sin).astype(BF16)

    @pl.when(jnp.logical_not(is_rope))
    def _():
        z_ref[...] = acc.astype(BF16)


def _in_proj(x, gain, mod_rows, layer, w_in, cos, sin, lay):
    t, d = x.shape
    n = w_in.shape[1]
    tm, tn = INPROJ_TM, INPROJ_TN
    nseq = lay[0] + lay[2]
    assert COL_Q % tn == 0 and COL_V % tn == 0

    def mod_idx(which):
        def idx(i, j):
            seq, _, _ = _locate(i * tm, lay)
            return ((layer * nseq + seq) * 3 + which, 0, 0)
        return idx

    def rope_idx(i, j):
        _, local, _ = _locate(i * tm, lay)
        return (local // tm, 0)

    return pl.pallas_call(
        functools.partial(_inproj_kernel, rope_lo=COL_Q // tn, rope_hi=COL_V // tn),
        grid=(t // tm, n // tn),
        in_specs=[
            pl.BlockSpec((tm, d), lambda i, j: (i, 0)),
            pl.BlockSpec((1, d), lambda i, j: (0, 0)),
            pl.BlockSpec((1, 1, d), mod_idx(0)),
            pl.BlockSpec((1, 1, d), mod_idx(1)),
            pl.BlockSpec((d, tn), lambda i, j: (0, j)),
            pl.BlockSpec((tm, HEAD_DIM), rope_idx),
            pl.BlockSpec((tm, HEAD_DIM), rope_idx),
        ],
        out_specs=pl.BlockSpec((tm, tn), lambda i, j: (i, j)),
        out_shape=jax.ShapeDtypeStruct((t, n), BF16),
        scratch_shapes=[pltpu.VMEM((tm, d), BF16)],
        compiler_params=pltpu.CompilerParams(
            dimension_semantics=("parallel", "arbitrary"), vmem_limit_bytes=VMEM_LIMIT_BYTES),
        name="in_proj",
    )(x, gain.reshape(1, d), mod_rows, mod_rows, w_in, cos, sin)


def _attn_kernel(q_ref, kp_ref, km_ref, kn_ref, vp_ref, vm_ref, vn_ref, o_ref, lse_ref, k_buf, v_buf, *, dil, lay):
    bq = q_ref.shape[0]
    r = BAND_RADIUS
    k_buf[0:r] = kp_ref[...]
    k_buf[r:r + bq] = km_ref[...]
    k_buf[r + bq:] = kn_ref[...]
    v_buf[0:r] = vp_ref[...]
    v_buf[r:r + bq] = vm_ref[...]
    v_buf[r + bq:] = vn_ref[...]

    nk = ATTN_SUB + 2 * r
    qi = lax.broadcasted_iota(jnp.int32, (ATTN_SUB, nk), 0)
    kj = lax.broadcasted_iota(jnp.int32, (ATTN_SUB, nk), 1) - r
    in_band = jnp.abs(kj - qi) <= r
    lane = lax.broadcasted_iota(jnp.int32, (ATTN_SUB, HEAD_DIM), 1)
    inv_sqrt = 1.0 / math.sqrt(HEAD_DIM)

    for sb in range(bq // ATTN_SUB):
        a = sb * ATTN_SUB
        _, local, length = _locate(pl.program_id(1) * bq + a, lay, dil)
        mask = in_band & (kj + local >= 0) & (kj + local < length)
        lse_tile = jnp.zeros((ATTN_SUB, HEAD_DIM), F32)
        for hh in range(HEADS_PER_GROUP):
            cs = slice(hh * HEAD_DIM, (hh + 1) * HEAD_DIM)
            q = q_ref[a:a + ATTN_SUB, cs]
            k = k_buf[a:a + nk, cs]
            v = v_buf[a:a + nk, cs]
            s = lax.dot_general(q, k, (((1,), (1,)), ((), ())), preferred_element_type=F32) * inv_sqrt
            s = jnp.where(mask, s, NEG_BIG)
            m = jnp.max(s, axis=-1, keepdims=True)
            p = jnp.exp(s - m)
            l = jnp.sum(p, axis=-1, keepdims=True)
            o = jnp.dot(p.astype(BF16), v, preferred_element_type=F32) * (1.0 / l)
            o_ref[a:a + ATTN_SUB, cs] = o.astype(BF16)
            lse_tile = jnp.where(lane == hh, m + jnp.log(l), lse_tile)
        lse_ref[a:a + ATTN_SUB, :] = lse_tile


def _band_attention(z, group, dil, lay):
    t = z.shape[0]
    rows = t // dil
    bq, r, gw = ATTN_BQ, BAND_RADIUS, GROUP_WIDTH
    assert rows % bq == 0 and bq % r == 0
    zr = z.reshape(rows, dil * IN_WIDTH)
    per_class = IN_WIDTH // gw
    qc, kc, vc = COL_Q // gw + group, COL_K // gw + group, COL_V // gw + group
    halo_last = rows // r - 1

    def main(col):
        return pl.BlockSpec((bq, gw), lambda c, n: (n, c * per_class + col))

    def prev(col):
        return pl.BlockSpec((r, gw), lambda c, n: (jnp.maximum(n * (bq // r) - 1, 0), c * per_class + col))

    def nxt(col):
        return pl.BlockSpec((r, gw), lambda c, n: (jnp.minimum((n + 1) * (bq // r), halo_last), c * per_class + col))

    o, lse = pl.pallas_call(
        functools.partial(_attn_kernel, dil=dil, lay=lay),
        grid=(dil, rows // bq),
        in_specs=[main(qc), prev(kc), main(kc), nxt(kc), prev(vc), main(vc), nxt(vc)],
        out_specs=[
            pl.BlockSpec((bq, gw), lambda c, n: (n, c)),
            pl.BlockSpec((bq, HEAD_DIM), lambda c, n: (n, c)),
        ],
        out_shape=[
            jax.ShapeDtypeStruct((rows, dil * gw), BF16),
            jax.ShapeDtypeStruct((rows, dil * HEAD_DIM), F32),
        ],
        scratch_shapes=[pltpu.VMEM((bq + 2 * r, gw), BF16), pltpu.VMEM((bq + 2 * r, gw), BF16)],
        compiler_params=pltpu.CompilerParams(
            dimension_semantics=("parallel", "parallel"), vmem_limit_bytes=VMEM_LIMIT_BYTES),
        name=f"band_attn_d{dil}",
    )(zr, zr, zr, zr, zr, zr, zr)
    return o.reshape(t, gw), lse.reshape(t, HEAD_DIM)


def _merge_kernel(pin_ref, pprev_ref, pnext_ref, pgate_ref, ag0_ref, ag1_ref, ag2_ref, gpool_ref, gattn_ref,
                  o0_ref, o1_ref, o2_ref, l0_ref, l1_ref, l2_ref, x_ref, gate_ref,
                  wgrp_ref, pscale_ref, wpp_ref, wpa_ref, wout_ref,
                  out_ref, ext_ref, a_ref, b_ref, *, lay):
    tm = x_ref.shape[0]
    hl = POOL_HALO
    _, t0, length = _locate(pl.program_id(0) * tm, lay)

    prev = pprev_ref[...].astype(F32)[BF16_SUBLANES - hl:]
    nxt = pnext_ref[...].astype(F32)[:hl]
    ext_ref[0:hl] = jnp.where(t0 == 0, 0.0, prev)
    ext_ref[hl:hl + tm] = pin_ref[...].astype(F32)
    ext_ref[hl + tm:] = jnp.where(t0 + tm == length, 0.0, nxt)
    t = t0 + lax.broadcasted_iota(jnp.int32, (tm, 1), 0)
    for g, w in enumerate(POOL_WINDOWS):
        cs = slice(g * POOL_GROUP, (g + 1) * POOL_GROUP)
        acc = ext_ref[hl - w // 2:hl - w // 2 + tm, cs]
        for k in range(-(w // 2) + 1, w // 2):
            acc = acc + ext_ref[hl + k:hl + k + tm, cs]
        cnt = (jnp.minimum(t + w // 2, length) - jnp.maximum(t - w // 2, 0)).astype(F32)
        p = acc * (1.0 / cnt) - ext_ref[hl:hl + tm, cs]
        y = jnp.dot(p.astype(BF16), wgrp_ref[g], preferred_element_type=F32)
        a = y * pscale_ref[:, cs] * _silu(pgate_ref[:, cs].astype(F32))
        a_ref[:, cs] = a.astype(BF16)

    lses = (l0_ref[...], l1_ref[...], l2_ref[...])
    mx = jnp.maximum(jnp.maximum(lses[0], lses[1]), lses[2])
    es = [jnp.exp(l - mx) for l in lses]
    inv_den = 1.0 / (es[0] + es[1] + es[2])
    for g, (o_ref, ag_ref) in enumerate(((o0_ref, ag0_ref), (o1_ref, ag1_ref), (o2_ref, ag2_ref))):
        wts = es[g] * inv_den
        for hh in range(HEADS_PER_GROUP):
            cs = slice(hh * HEAD_DIM, (hh + 1) * HEAD_DIM)
            b = o_ref[:, cs].astype(F32) * wts[:, hh:hh + 1] * _silu(ag_ref[:, cs].astype(F32))
            b_ref[:, g * GROUP_WIDTH + hh * HEAD_DIM:g * GROUP_WIDTH + (hh + 1) * HEAD_DIM] = b.astype(BF16)

    pp = jnp.dot(a_ref[...], wpp_ref[...], preferred_element_type=F32)
    pa = jnp.dot(b_ref[...], wpa_ref[...], preferred_element_type=F32)
    merged = _sigmoid(gpool_ref[...].astype(F32)) * pp + _sigmoid(gattn_ref[...].astype(F32)) * pa
    out = jnp.dot(merged.astype(BF16), wout_ref[...], preferred_element_type=F32)
    out_ref[...] = x_ref[...] + gate_ref[0] * out


def _merge(z, attn, x, mod_rows, layer, w_grp, pool_scale, w_pp, w_pa, w_out, lay):
    t, d = x.shape
    tm = MERGE_TM
    nseq = lay[0] + lay[2]
    hb = BF16_SUBLANES
    halo_last = t // hb - 1
    (o0, l0), (o1, l1), (o2, l2) = attn

    def zcols(width, col):
        assert col % width == 0
        return pl.BlockSpec((tm, width), lambda i: (i, col // width))

    def gate_idx(i):
        seq, _, _ = _locate(i * tm, lay)
        return ((layer * nseq + seq) * 3 + 2, 0, 0)

    def const(shape):
        return pl.BlockSpec(shape, lambda i: (0,) * len(shape), pipeline_mode=pl.Buffered(1))

    row = lambda width: pl.BlockSpec((tm, width), lambda i: (i, 0))
    return pl.pallas_call(
        functools.partial(_merge_kernel, lay=lay),
        grid=(t // tm,),
        in_specs=[
            zcols(POOL_WIDTH, COL_POOL_IN),
            pl.BlockSpec((hb, POOL_WIDTH), lambda i: (jnp.maximum(i * (tm // hb) - 1, 0), 0)),
            pl.BlockSpec((hb, POOL_WIDTH), lambda i: (jnp.minimum((i + 1) * (tm // hb), halo_last), 0)),
            zcols(POOL_WIDTH, COL_POOL_GATE),
            zcols(GROUP_WIDTH, COL_ATTN_GATE),
            zcols(GROUP_WIDTH, COL_ATTN_GATE + GROUP_WIDTH),
            zcols(GROUP_WIDTH, COL_ATTN_GATE + 2 * GROUP_WIDTH),
            zcols(D_MODEL, COL_G_POOL),
            zcols(D_MODEL, COL_G_ATTN),
            row(GROUP_WIDTH), row(GROUP_WIDTH), row(GROUP_WIDTH),
            row(HEAD_DIM), row(HEAD_DIM), row(HEAD_DIM),
            row(d),
            pl.BlockSpec((1, 1, d), gate_idx),
            const(w_grp.shape), const((1, POOL_WIDTH)), const(w_pp.shape), const(w_pa.shape), const(w_out.shape),
        ],
        out_specs=row(d),
        out_shape=jax.ShapeDtypeStruct((t, d), F32),
        scratch_shapes=[
            pltpu.VMEM((tm + 2 * POOL_HALO, POOL_WIDTH), F32),
            pltpu.VMEM((tm, POOL_WIDTH), BF16),
            pltpu.VMEM((tm, ATTN_WIDTH), BF16),
        ],
        compiler_params=pltpu.CompilerParams(
            dimension_semantics=("parallel",), vmem_limit_bytes=VMEM_LIMIT_BYTES),
        name="merge",
    )(z, z, z, z, z, z, z, z, z, o0, o1, o2, l0, l1, l2, x, mod_rows,
      w_grp, pool_scale.reshape(1, POOL_WIDTH), w_pp, w_pa, w_out)


def _norm_kernel(x_ref, g_ref, o_ref):
    x = x_ref[...]
    o_ref[...] = x * lax.rsqrt(jnp.mean(x * x, axis=-1, keepdims=True) + NORM_EPS) * g_ref[...]


def _final_norm(x, gain, row_start, rows):
    d = x.shape[1]
    tm = NORM_TM
    assert row_start % tm == 0 and rows % tm == 0
    return pl.pallas_call(
        _norm_kernel,
        grid=(rows // tm,),
        in_specs=[
            pl.BlockSpec((tm, d), lambda i: (i + row_start // tm, 0)),
            pl.BlockSpec((1, d), lambda i: (0, 0)),
        ],
        out_specs=pl.BlockSpec((tm, d), lambda i: (i, 0)),
        out_shape=jax.ShapeDtypeStruct((rows, d), F32),
        compiler_params=pltpu.CompilerParams(
            dimension_semantics=("parallel",), vmem_limit_bytes=VMEM_LIMIT_BYTES),
        name="final_norm",
    )(x, gain.reshape(1, d))


def kernel(x_prompt, x_sample, c_prompt, c_sample, norm_gain, w_ada, b_ada, w_in, w_pool_grp, pool_scale,
           w_proj_pool, w_proj_attn, w_out, final_gain):
    bp, sp, d = x_prompt.shape
    bs, ss, _ = x_sample.shape
    assert d == D_MODEL
    lay = (bp, sp, bs, ss)
    depth = w_in.shape[0]
    nseq = bp + bs
    max_dil = max(dil for _, dil in ATTN_GROUPS)
    for s in (sp, ss):
        assert s % INPROJ_TM == 0 and s % MERGE_TM == 0 and s % (max_dil * ATTN_SUB) == 0

    x = jnp.concatenate([x_prompt.reshape(bp * sp, d), x_sample.reshape(bs * ss, d)], axis=0)
    c = jnp.concatenate([c_prompt, c_sample], axis=0)
    mod_rows = _modulation(c, w_ada, b_ada).reshape(depth * nseq * 3, 1, d)
    cos, sin = _rope_tables(max(sp, ss))

    for layer in range(depth):
        z = _in_proj(x, norm_gain[layer], mod_rows, layer, w_in[layer].astype(BF16), cos, sin, lay)
        attn = [_band_attention(z, g, dil, lay) for g, (_, dil) in enumerate(ATTN_GROUPS)]
        x = _merge(z, attn, x, mod_rows, layer, w_pool_grp[layer].astype(BF16), pool_scale[layer],
                   w_proj_pool[layer].astype(BF16), w_proj_attn[layer].astype(BF16), w_out[layer].astype(BF16), lay)

    y_prompt = _final_norm(x, final_gain, 0, bp * sp).reshape(bp, sp, d)
    y_sample = _final_norm(x, final_gain, bp * sp, bs * ss).reshape(bs, ss, d)
    return (y_prompt, y_sample)
```

```python
import functools
import math

import jax
import jax.numpy as jnp
from jax import lax
from jax.experimental import pallas as pl
from jax.experimental.pallas import tpu as pltpu

F32 = jnp.float32
BF16 = jnp.bfloat16

D_MODEL = 2048
POOL_WIDTH = D_MODEL // 2
POOL_WINDOWS = (2, 4, 8, 16)
POOL_GROUP = POOL_WIDTH // len(POOL_WINDOWS)
POOL_HALO = max(POOL_WINDOWS) // 2
HEAD_DIM = 128
ATTN_GROUPS = ((128, 1), (512, 4), (2048, 16))
DILATIONS = tuple(dil for _, dil in ATTN_GROUPS)
HEADS_PER_GROUP = 4
GROUP_WIDTH = HEADS_PER_GROUP * HEAD_DIM
ATTN_WIDTH = GROUP_WIDTH * len(ATTN_GROUPS)
BAND_RADIUS = 64
assert all(win // (2 * dil) == BAND_RADIUS for win, dil in ATTN_GROUPS)
assert DILATIONS[0] == 1 and len(DILATIONS) == 3
ROPE_THETA = 10000.0
NORM_EPS = 1e-6
NEG_BIG = -1e30
IN_WIDTH = 2 * POOL_WIDTH + 4 * ATTN_WIDTH + 2 * D_MODEL

_CH_POOL_IN, _CH_POOL_GATE, _CH_Q, _CH_K, _CH_V, _CH_AG, _CH_GP, _CH_GA = 0, 2, 4, 7, 10, 13, 16, 20
IN_PROJ_CHUNK_ORDER = (
    _CH_Q, _CH_K, _CH_Q + 1, _CH_K + 1, _CH_Q + 2, _CH_K + 2, _CH_V + 1, _CH_V + 2,
    _CH_V, _CH_AG, *range(_CH_GP, _CH_GP + 4), *range(_CH_GA, _CH_GA + 4),
    _CH_POOL_IN, _CH_POOL_IN + 1, _CH_POOL_GATE, _CH_POOL_GATE + 1, _CH_AG + 1, _CH_AG + 2)
assert sorted(IN_PROJ_CHUNK_ORDER) == list(range(IN_WIDTH // GROUP_WIDTH))
N_SPECIAL_TILES = 4
Z_WIDTH = IN_WIDTH - 6 * GROUP_WIDTH
ZC_Q0, ZC_K0, ZC_V0, ZC_AG0 = 0, GROUP_WIDTH, 2 * GROUP_WIDTH, 3 * GROUP_WIDTH
ZC_G_POOL = 4 * GROUP_WIDTH
ZC_G_ATTN = ZC_G_POOL + D_MODEL
ZC_POOL_IN = ZC_G_ATTN + D_MODEL
ZC_POOL_GATE = ZC_POOL_IN + POOL_WIDTH
ZC_AG1 = ZC_POOL_GATE + POOL_WIDTH
ZC_AG2 = ZC_AG1 + GROUP_WIDTH

VMEM_LIMIT_BYTES = 56 * 1024 * 1024
BF16_SUBLANES = 16

INPROJ_TM = 1024
INPROJ_TN = 2 * GROUP_WIDTH
ATTN_BQ = 512
ATTN_SUB = 128
MERGE_TM = 256
NORM_TM = 512


def _sigmoid(x):
    return 1.0 / (1.0 + jnp.exp(-x))


def _silu(x):
    return x * _sigmoid(x)


def _mod_norm(x, gain, scale, shift):
    xn = x * lax.rsqrt(jnp.mean(x * x, axis=-1, keepdims=True) + NORM_EPS) * gain
    return xn * (1.0 + scale) + shift


def _locate(row0, lay, dil=1):
    bp, sp, bs, ss = lay
    rows_p, lp, ls = bp * sp // dil, sp // dil, ss // dil
    in_p = row0 < rows_p
    seq = jnp.where(in_p, row0 // lp, bp + (row0 - rows_p) // ls)
    local = jnp.where(in_p, row0 % lp, (row0 - rows_p) % ls)
    length = jnp.where(in_p, lp, ls)
    return seq, local, length


def _mod_spec(layer, which, tm, lay, block_off=0):
    nseq = lay[0] + lay[2]

    def idx(i, *_):
        seq, _, _ = _locate((i + block_off) * tm, lay)
        return ((layer * nseq + seq) * 3 + which, 0, 0)

    return pl.BlockSpec((1, 1, D_MODEL), idx)


def _mod_kernel(c_ref, w_ref, b_ref, o_ref):
    c = c_ref[...].astype(BF16)
    w = w_ref[0].astype(BF16)
    o_ref[0] = jnp.dot(c, w, preferred_element_type=F32) + b_ref[0]


def _modulation(c, w_ada, b_ada):
    depth, d, n = w_ada.shape
    nseq = c.shape[0]
    tn = 1536
    return pl.pallas_call(
        _mod_kernel,
        grid=(depth, n // tn),
        in_specs=[
            pl.BlockSpec((nseq, d), lambda l, j: (0, 0)),
            pl.BlockSpec((1, d, tn), lambda l, j: (l, 0, j)),
            pl.BlockSpec((1, 1, tn), lambda l, j: (l, 0, j)),
        ],
        out_specs=pl.BlockSpec((1, nseq, tn), lambda l, j: (l, 0, j)),
        out_shape=jax.ShapeDtypeStruct((depth, nseq, n), F32),
        compiler_params=pltpu.CompilerParams(
            dimension_semantics=("arbitrary", "arbitrary"), vmem_limit_bytes=VMEM_LIMIT_BYTES),
        name="modulation",
    )(c, w_ada, b_ada.reshape(depth, 1, n))


def _rope_table_kernel(cos_ref, sin_ref):
    s = cos_ref.shape[0]
    half = HEAD_DIM // 2
    pos = lax.broadcasted_iota(jnp.int32, (s, HEAD_DIM), 0).astype(F32)
    lane = lax.broadcasted_iota(jnp.int32, (s, HEAD_DIM), 1)
    idx = jnp.where(lane < half, lane, lane - half).astype(F32)
    inv = jnp.exp(idx * (-math.log(ROPE_THETA) / half))
    ang = pos * inv
    cos_ref[...] = jnp.cos(ang)
    sin_ref[...] = jnp.where(lane < half, -jnp.sin(ang), jnp.sin(ang))


def _rope_tables(s):
    return pl.pallas_call(
        _rope_table_kernel,
        out_shape=(jax.ShapeDtypeStruct((s, HEAD_DIM), F32), jax.ShapeDtypeStruct((s, HEAD_DIM), F32)),
        name="rope_tables",
    )()


def _prenorm_kernel(x_ref, g_ref, shift_ref, scale_ref, h_ref):
    h_ref[...] = _mod_norm(x_ref[...], g_ref[...], scale_ref[0], shift_ref[0]).astype(BF16)


def _prenorm(x, gain, mod_rows, lay):
    t, d = x.shape
    tm = NORM_TM
    return pl.pallas_call(
        _prenorm_kernel,
        grid=(t // tm,),
        in_specs=[
            pl.BlockSpec((tm, d), lambda i: (i, 0)),
            pl.BlockSpec((1, d), lambda i: (0, 0)),
            _mod_spec(0, 0, tm, lay),
            _mod_spec(0, 1, tm, lay),
        ],
        out_specs=pl.BlockSpec((tm, d), lambda i: (i, 0)),
        out_shape=jax.ShapeDtypeStruct((t, d), BF16),
        compiler_params=pltpu.CompilerParams(
            dimension_semantics=("parallel",), vmem_limit_bytes=VMEM_LIMIT_BYTES),
        name="prenorm",
    )(x, gain.reshape(1, d), mod_rows, mod_rows)


def _inproj_kernel(h_ref, w_ref, cos_ref, sin_ref, z_ref, qk1_ref, qk2_ref, v1_ref, v2_ref, s_ref):
    j = pl.program_id(1)
    n_chunks, tm, _ = s_ref.shape
    d1, d2 = DILATIONS[1], DILATIONS[2]
    hpg = HEADS_PER_GROUP
    acc = jnp.dot(h_ref[...], w_ref[...], preferred_element_type=F32)

    def chunk(hh):
        return slice(hh * HEAD_DIM, (hh + 1) * HEAD_DIM)

    def roped(hh):
        xh = acc[:, chunk(hh)]
        return xh * cos_ref[...] + pltpu.roll(xh, HEAD_DIM // 2, 1) * sin_ref[...]

    def to_classes(dst_ref, dil, chunks):
        for c_out, hh in enumerate(chunks):
            for r in range(dil):
                dst_ref[r, :, chunk(c_out)] = s_ref[hh, pl.ds(r, tm // dil, stride=dil), :].astype(BF16)

    @pl.when(j == 0)
    def _():
        for hh in range(n_chunks):
            z_ref[:, chunk(hh)] = roped(hh).astype(BF16)

    @pl.when(j == 1)
    def _():
        for hh in range(n_chunks):
            s_ref[hh] = roped(hh)
        to_classes(qk1_ref, d1, range(n_chunks))

    @pl.when(j == 2)
    def _():
        for hh in range(n_chunks):
            s_ref[hh] = roped(hh)
        to_classes(qk2_ref, d2, range(n_chunks))

    @pl.when(j == 3)
    def _():
        for hh in range(n_chunks):
            s_ref[hh] = acc[:, chunk(hh)]
        to_classes(v1_ref, d1, range(0, hpg))
        to_classes(v2_ref, d2, range(hpg, 2 * hpg))

    @pl.when(j >= N_SPECIAL_TILES)
    def _():
        z_ref[...] = acc.astype(BF16)


def _in_proj(h, w_in, cos, sin, lay):
    t, d = h.shape
    n = w_in.shape[1]
    tm, tn, gw = INPROJ_TM, INPROJ_TN, GROUP_WIDTH
    d1, d2 = DILATIONS[1], DILATIONS[2]

    def rope_idx(i, j):
        _, local, _ = _locate(i * tm, lay)
        return (local // tm, 0)

    def cls_spec(dil, width):
        return pl.BlockSpec((dil, tm // dil, width), lambda i, j: (0, i, 0))

    def cls_shape(dil, width):
        return jax.ShapeDtypeStruct((dil, t // dil, width), BF16)

    return pl.pallas_call(
        _inproj_kernel,
        grid=(t // tm, n // tn),
        in_specs=[
            pl.BlockSpec((tm, d), lambda i, j: (i, 0)),
            pl.BlockSpec((d, tn), lambda i, j: (0, j)),
            pl.BlockSpec((tm, HEAD_DIM), rope_idx),
            pl.BlockSpec((tm, HEAD_DIM), rope_idx),
        ],
        out_specs=[
            pl.BlockSpec((tm, tn), lambda i, j: (i, jnp.maximum(j - (N_SPECIAL_TILES - 1), 0))),
            cls_spec(d1, 2 * gw), cls_spec(d2, 2 * gw), cls_spec(d1, gw), cls_spec(d2, gw),
        ],
        out_shape=[
            jax.ShapeDtypeStruct((t, Z_WIDTH), BF16),
            cls_shape(d1, 2 * gw), cls_shape(d2, 2 * gw), cls_shape(d1, gw), cls_shape(d2, gw),
        ],
        scratch_shapes=[pltpu.VMEM((tn // HEAD_DIM, tm, HEAD_DIM), F32)],
        compiler_params=pltpu.CompilerParams(
            dimension_semantics=("parallel", "arbitrary"), vmem_limit_bytes=VMEM_LIMIT_BYTES),
        name="in_proj",
    )(h, w_in, cos, sin)


def _attn_kernel(q_ref, kp_ref, km_ref, kn_ref, vp_ref, vm_ref, vn_ref, o_ref, lse_ref, k_buf, v_buf, *, dil, lay):
    bq = q_ref.shape[0]
    r = BAND_RADIUS
    k_buf[0:r] = kp_ref[...]
    k_buf[r:r + bq] = km_ref[...]
    k_buf[r + bq:] = kn_ref[...]
    v_buf[0:r] = vp_ref[...]
    v_buf[r:r + bq] = vm_ref[...]
    v_buf[r + bq:] = vn_ref[...]

    nk = ATTN_SUB + 2 * r
    qi = lax.broadcasted_iota(jnp.int32, (ATTN_SUB, nk), 0)
    kj = lax.broadcasted_iota(jnp.int32, (ATTN_SUB, nk), 1) - r
    in_band = jnp.abs(kj - qi) <= r
    lane = lax.broadcasted_iota(jnp.int32, (ATTN_SUB, HEAD_DIM), 1)
    inv_sqrt = 1.0 / math.sqrt(HEAD_DIM)

    for sb in range(bq // ATTN_SUB):
        a = sb * ATTN_SUB
        _, local, length = _locate(pl.program_id(1) * bq + a, lay, dil)
        mask = in_band & (kj + local >= 0) & (kj + local < length)
        lse_tile = jnp.zeros((ATTN_SUB, HEAD_DIM), F32)
        for hh in range(HEADS_PER_GROUP):
            cs = slice(hh * HEAD_DIM, (hh + 1) * HEAD_DIM)
            q = q_ref[a:a + ATTN_SUB, cs]
            k = k_buf[a:a + nk, cs]
            v = v_buf[a:a + nk, cs]
            s = lax.dot_general(q, k, (((1,), (1,)), ((), ())), preferred_element_type=F32) * inv_sqrt
            s = jnp.where(mask, s, NEG_BIG)
            m = jnp.max(s, axis=-1, keepdims=True)
            p = jnp.exp(s - m)
            l = jnp.sum(p, axis=-1, keepdims=True)
            o = jnp.dot(p.astype(BF16), v, preferred_element_type=F32) * (1.0 / l)
            o_ref[a:a + ATTN_SUB, cs] = o.astype(BF16)
            lse_tile = jnp.where(lane == hh, m + jnp.log(l), lse_tile)
        lse_ref[a:a + ATTN_SUB, :] = lse_tile


def _band_attention(qk, v, q_col, k_col, v_col, dil, lay):
    rows = qk.shape[1]
    bq, r, gw = ATTN_BQ, BAND_RADIUS, GROUP_WIDTH
    assert rows % bq == 0 and bq % r == 0
    halo_last = rows // r - 1

    def main(col):
        return pl.BlockSpec((None, bq, gw), lambda c, n: (c, n, col))

    def prev(col):
        return pl.BlockSpec((None, r, gw), lambda c, n: (c, jnp.maximum(n * (bq // r) - 1, 0), col))

    def nxt(col):
        return pl.BlockSpec((None, r, gw), lambda c, n: (c, jnp.minimum((n + 1) * (bq // r), halo_last), col))

    return pl.pallas_call(
        functools.partial(_attn_kernel, dil=dil, lay=lay),
        grid=(dil, rows // bq),
        in_specs=[main(q_col), prev(k_col), main(k_col), nxt(k_col), prev(v_col), main(v_col), nxt(v_col)],
        out_specs=[
            pl.BlockSpec((None, bq, gw), lambda c, n: (c, n, 0)),
            pl.BlockSpec((None, bq, HEAD_DIM), lambda c, n: (c, n, 0)),
        ],
        out_shape=[
            jax.ShapeDtypeStruct((dil, rows, gw), BF16),
            jax.ShapeDtypeStruct((dil, rows, HEAD_DIM), F32),
        ],
        scratch_shapes=[pltpu.VMEM((bq + 2 * r, gw), BF16), pltpu.VMEM((bq + 2 * r, gw), BF16)],
        compiler_params=pltpu.CompilerParams(
            dimension_semantics=("parallel", "parallel"), vmem_limit_bytes=VMEM_LIMIT_BYTES),
        name=f"band_attn_d{dil}",
    )(qk, qk, qk, qk, v, v, v)


def _merge_kernel(pin_ref, pprev_ref, pnext_ref, pgate_ref, ag0_ref, ag1_ref, ag2_ref, gpool_ref, gattn_ref,
                  o0_ref, o1_ref, o2_ref, l0_ref, l1_ref, l2_ref, x_ref, gate_ref,
                  wgrp_ref, pscale_ref, wpp_ref, wpa_ref, wout_ref, ngain_ref, nshift_ref, nscale_ref,
                  *rest, lay, block_off, final):
    if final:
        y_ref, ext_ref, a_ref, b_ref, oi_ref, li_ref = rest
    else:
        out_ref, h_ref, ext_ref, a_ref, b_ref, oi_ref, li_ref = rest
    tm = x_ref.shape[0]
    hl = POOL_HALO
    gw = GROUP_WIDTH
    _, t0, length = _locate((pl.program_id(0) + block_off) * tm, lay)

    prev = pprev_ref[...].astype(F32)[BF16_SUBLANES - hl:]
    nxt = pnext_ref[...].astype(F32)[:hl]
    ext_ref[0:hl] = jnp.where(t0 == 0, 0.0, prev)
    ext_ref[hl:hl + tm] = pin_ref[...].astype(F32)
    ext_ref[hl + tm:] = jnp.where(t0 + tm == length, 0.0, nxt)
    t = t0 + lax.broadcasted_iota(jnp.int32, (tm, 1), 0)
    for g, w in enumerate(POOL_WINDOWS):
        cs = slice(g * POOL_GROUP, (g + 1) * POOL_GROUP)
        acc = ext_ref[hl - w // 2:hl - w // 2 + tm, cs]
        for k in range(-(w // 2) + 1, w // 2):
            acc = acc + ext_ref[hl + k:hl + k + tm, cs]
        cnt = (jnp.minimum(t + w // 2, length) - jnp.maximum(t - w // 2, 0)).astype(F32)
        p = acc * (1.0 / cnt) - ext_ref[hl:hl + tm, cs]
        y = jnp.dot(p.astype(BF16), wgrp_ref[g], preferred_element_type=F32)
        a = y * pscale_ref[:, cs] * _silu(pgate_ref[:, cs].astype(F32))
        a_ref[:, cs] = a.astype(BF16)

    hpg = HEADS_PER_GROUP
    for slot, (o_ref, l_ref, dil) in enumerate(((o1_ref, l1_ref, DILATIONS[1]), (o2_ref, l2_ref, DILATIONS[2]))):
        for r in range(dil):
            rows = pl.ds(r, tm // dil, stride=dil)
            for hh in range(hpg):
                oi_ref[slot * hpg + hh, rows, :] = o_ref[r, :, hh * HEAD_DIM:(hh + 1) * HEAD_DIM].astype(F32)
            li_ref[slot, rows, :] = l_ref[r]

    lses = (l0_ref[...], li_ref[0], li_ref[1])
    mx = jnp.maximum(jnp.maximum(lses[0], lses[1]), lses[2])
    es = [jnp.exp(l - mx) for l in lses]
    inv_den = 1.0 / (es[0] + es[1] + es[2])
    for g, ag_ref in enumerate((ag0_ref, ag1_ref, ag2_ref)):
        wts = es[g] * inv_den
        for hh in range(HEADS_PER_GROUP):
            cs = slice(hh * HEAD_DIM, (hh + 1) * HEAD_DIM)
            if g == 0:
                o = o0_ref[:, cs].astype(F32)
            else:
                o = oi_ref[(g - 1) * hpg + hh]
            b = o * wts[:, hh:hh + 1] * _silu(ag_ref[:, cs].astype(F32))
            b_ref[:, g * gw + hh * HEAD_DIM:g * gw + (hh + 1) * HEAD_DIM] = b.astype(BF16)

    pp = jnp.dot(a_ref[...], wpp_ref[...], preferred_element_type=F32)
    pa = jnp.dot(b_ref[...], wpa_ref[...], preferred_element_type=F32)
    merged = _sigmoid(gpool_ref[...].astype(F32)) * pp + _sigmoid(gattn_ref[...].astype(F32)) * pa
    out = jnp.dot(merged.astype(BF16), wout_ref[...], preferred_element_type=F32)
    x_new = x_ref[...] + gate_ref[0] * out
    if final:
        y_ref[...] = x_new * lax.rsqrt(jnp.mean(x_new * x_new, axis=-1, keepdims=True) + NORM_EPS) * ngain_ref[...]
    else:
        out_ref[...] = x_new
        h_ref[...] = _mod_norm(x_new, ngain_ref[...], nscale_ref[0], nshift_ref[0]).astype(BF16)


def _merge(z, attn, x, mod_rows, layer, w_grp, pool_scale, w_pp, w_pa, w_out, next_gain, lay, *,
           final=False, row_start=0, rows=None):
    t, d = x.shape
    rows = t if rows is None else rows
    tm = MERGE_TM
    hb = BF16_SUBLANES
    assert row_start % tm == 0 and rows % tm == 0
    off = row_start // tm
    halo_last = t // hb - 1
    (o0, l0), (o1, l1), (o2, l2) = attn
    d1, d2 = DILATIONS[1], DILATIONS[2]

    def zcols(width, col):
        assert col % width == 0
        return pl.BlockSpec((tm, width), lambda i: (i + off, col // width))

    def const(shape):
        return pl.BlockSpec(shape, lambda i: (0,) * len(shape), pipeline_mode=pl.Buffered(1))

    def row(width):
        return pl.BlockSpec((tm, width), lambda i: (i + off, 0))

    def cls(dil, width):
        return pl.BlockSpec((dil, tm // dil, width), lambda i: (0, i + off, 0))

    next_layer = layer if final else layer + 1
    out_row = pl.BlockSpec((tm, d), lambda i: (i, 0))
    if final:
        out_specs = out_row
        out_shape = jax.ShapeDtypeStruct((rows, d), F32)
    else:
        out_specs = [out_row, out_row]
        out_shape = [jax.ShapeDtypeStruct((rows, d), F32), jax.ShapeDtypeStruct((rows, d), BF16)]

    return pl.pallas_call(
        functools.partial(_merge_kernel, lay=lay, block_off=off, final=final),
        grid=(rows // tm,),
        in_specs=[
            zcols(POOL_WIDTH, ZC_POOL_IN),
            pl.BlockSpec((hb, POOL_WIDTH),
                         lambda i: (jnp.maximum((i + off) * (tm // hb) - 1, 0), ZC_POOL_IN // POOL_WIDTH)),
            pl.BlockSpec((hb, POOL_WIDTH),
                         lambda i: (jnp.minimum((i + off + 1) * (tm // hb), halo_last), ZC_POOL_IN // POOL_WIDTH)),
            zcols(POOL_WIDTH, ZC_POOL_GATE),
            zcols(GROUP_WIDTH, ZC_AG0),
            zcols(GROUP_WIDTH, ZC_AG1),
            zcols(GROUP_WIDTH, ZC_AG2),
            zcols(D_MODEL, ZC_G_POOL),
            zcols(D_MODEL, ZC_G_ATTN),
            row(GROUP_WIDTH), cls(d1, GROUP_WIDTH), cls(d2, GROUP_WIDTH),
            row(HEAD_DIM), cls(d1, HEAD_DIM), cls(d2, HEAD_DIM),
            row(d),
            _mod_spec(layer, 2, tm, lay, off),
            const(w_grp.shape), const((1, POOL_WIDTH)), const(w_pp.shape), const(w_pa.shape), const(w_out.shape),
            const((1, d)),
            _mod_spec(next_layer, 0, tm, lay, off),
            _mod_spec(next_layer, 1, tm, lay, off),
        ],
        out_specs=out_specs,
        out_shape=out_shape,
        scratch_shapes=[
            pltpu.VMEM((tm + 2 * POOL_HALO, POOL_WIDTH), F32),
            pltpu.VMEM((tm, POOL_WIDTH), BF16),
            pltpu.VMEM((tm, ATTN_WIDTH), BF16),
            pltpu.VMEM((2 * HEADS_PER_GROUP, tm, HEAD_DIM), F32),
            pltpu.VMEM((2, tm, HEAD_DIM), F32),
        ],
        compiler_params=pltpu.CompilerParams(
            dimension_semantics=("parallel",), vmem_limit_bytes=VMEM_LIMIT_BYTES),
        name="merge_final" if final else "merge",
    )(z, z, z, z, z, z, z, z, z, o0, o1, o2, l0, l1, l2, x, mod_rows,
      w_grp, pool_scale.reshape(1, POOL_WIDTH), w_pp, w_pa, w_out, next_gain.reshape(1, d), mod_rows, mod_rows)


def _permute_in_proj_columns(w):
    gw = GROUP_WIDTH
    return jnp.concatenate([w[:, c * gw:(c + 1) * gw] for c in IN_PROJ_CHUNK_ORDER], axis=1)


def kernel(x_prompt, x_sample, c_prompt, c_sample, norm_gain, w_ada, b_ada, w_in, w_pool_grp, pool_scale,
           w_proj_pool, w_proj_attn, w_out, final_gain):
    bp, sp, d = x_prompt.shape
    bs, ss, _ = x_sample.shape
    assert d == D_MODEL
    lay = (bp, sp, bs, ss)
    depth = w_in.shape[0]
    nseq = bp + bs
    t = bp * sp + bs * ss
    gw = GROUP_WIDTH
    for s in (sp, ss):
        assert s % INPROJ_TM == 0 and s % MERGE_TM == 0 and s % (max(DILATIONS) * ATTN_SUB) == 0

    x = jnp.concatenate([x_prompt.reshape(bp * sp, d), x_sample.reshape(bs * ss, d)], axis=0)
    c = jnp.concatenate([c_prompt, c_sample], axis=0)
    mod_rows = _modulation(c, w_ada, b_ada).reshape(depth * nseq * 3, 1, d)
    cos, sin = _rope_tables(max(sp, ss))

    h = _prenorm(x, norm_gain[0], mod_rows, lay)
    for layer in range(depth):
        w_in_l = _permute_in_proj_columns(w_in[layer]).astype(BF16)
        z, qk1, qk2, v1, v2 = _in_proj(h, w_in_l, cos, sin, lay)
        o0, l0 = _band_attention(z[None], z[None], ZC_Q0 // gw, ZC_K0 // gw, ZC_V0 // gw, DILATIONS[0], lay)
        attn = [
            (o0[0], l0[0]),
            _band_attention(qk1, v1, 0, 1, 0, DILATIONS[1], lay),
            _band_attention(qk2, v2, 0, 1, 0, DILATIONS[2], lay),
        ]
        weights = (w_pool_grp[layer].astype(BF16), pool_scale[layer], w_proj_pool[layer].astype(BF16),
                   w_proj_attn[layer].astype(BF16), w_out[layer].astype(BF16))
        if layer + 1 < depth:
            x, h = _merge(z, attn, x, mod_rows, layer, *weights, norm_gain[layer + 1], lay)
        else:
            y_prompt = _merge(z, attn, x, mod_rows, layer, *weights, final_gain, lay,
                              final=True, row_start=0, rows=bp * sp)
            y_sample = _merge(z, attn, x, mod_rows, layer, *weights, final_gain, lay,
                              final=True, row_start=bp * sp, rows=bs * ss)
    return (y_prompt.reshape(bp, sp, d), y_sample.reshape(bs, ss, d))
```

```python
import functools
import math

import jax
import jax.numpy as jnp
from jax import lax
from jax.experimental import pallas as pl
from jax.experimental.pallas import tpu as pltpu

F32 = jnp.float32
BF16 = jnp.bfloat16

D_MODEL = 2048
POOL_WIDTH = D_MODEL // 2
POOL_WINDOWS = (2, 4, 8, 16)
POOL_GROUP = POOL_WIDTH // len(POOL_WINDOWS)
POOL_HALO = 64
assert POOL_HALO >= max(POOL_WINDOWS) // 2
HEAD_DIM = 128
ATTN_GROUPS = ((128, 1), (512, 4), (2048, 16))
DILATIONS = tuple(dil for _, dil in ATTN_GROUPS)
HEADS_PER_GROUP = 4
GROUP_WIDTH = HEADS_PER_GROUP * HEAD_DIM
ATTN_WIDTH = GROUP_WIDTH * len(ATTN_GROUPS)
BAND_RADIUS = 64
assert all(win // (2 * dil) == BAND_RADIUS for win, dil in ATTN_GROUPS)
assert DILATIONS[0] == 1 and len(DILATIONS) == 3
ROPE_THETA = 10000.0
NORM_EPS = 1e-6
NEG_BIG = -1e30
IN_WIDTH = 2 * POOL_WIDTH + 4 * ATTN_WIDTH + 2 * D_MODEL

_CH_POOL_IN, _CH_POOL_GATE, _CH_Q, _CH_K, _CH_V, _CH_AG, _CH_GP, _CH_GA = 0, 2, 4, 7, 10, 13, 16, 20
IN_PROJ_CHUNK_ORDER = (
    _CH_Q, _CH_K, _CH_Q + 1, _CH_K + 1, _CH_Q + 2, _CH_K + 2, _CH_V + 1, _CH_V + 2,
    _CH_V, _CH_AG, *range(_CH_GP, _CH_GP + 4), *range(_CH_GA, _CH_GA + 4),
    _CH_POOL_IN, _CH_POOL_IN + 1, _CH_POOL_GATE, _CH_POOL_GATE + 1, _CH_AG + 1, _CH_AG + 2)
assert sorted(IN_PROJ_CHUNK_ORDER) == list(range(IN_WIDTH // GROUP_WIDTH))
HALVED_CHUNKS = frozenset((*range(_CH_POOL_GATE, _CH_POOL_GATE + 2), *range(_CH_AG, _CH_AG + 3),
                           *range(_CH_GP, _CH_GP + 4), *range(_CH_GA, _CH_GA + 4)))
N_SPECIAL_TILES = 4
Z_WIDTH = IN_WIDTH - 6 * GROUP_WIDTH
ZC_Q0, ZC_K0, ZC_V0, ZC_AG0 = 0, GROUP_WIDTH, 2 * GROUP_WIDTH, 3 * GROUP_WIDTH
ZC_G_POOL = 4 * GROUP_WIDTH
ZC_G_ATTN = ZC_G_POOL + D_MODEL
ZC_POOL_IN = ZC_G_ATTN + D_MODEL
ZC_POOL_GATE = ZC_POOL_IN + POOL_WIDTH
ZC_AG1 = ZC_POOL_GATE + POOL_WIDTH
ZC_AG2 = ZC_AG1 + GROUP_WIDTH

VMEM_LIMIT_BYTES = 56 * 1024 * 1024

INPROJ_TM = 1024
INPROJ_TN = 2 * GROUP_WIDTH
INPROJ_PIECE = 256
ATTN_BQ = 512
ATTN_SUB = 128
MERGE_TM = 256
MERGE_PIECE = 256
NORM_TM = 512


def _one_plus_tanh(x_half):
    return 1.0 + jnp.tanh(x_half)


def _mod_norm(x, gain, scale, shift):
    xn = x * lax.rsqrt(jnp.mean(x * x, axis=-1, keepdims=True) + NORM_EPS) * gain
    return xn * (1.0 + scale) + shift


def _locate(row0, lay, dil=1):
    bp, sp, bs, ss = lay
    rows_p, lp, ls = bp * sp // dil, sp // dil, ss // dil
    in_p = row0 < rows_p
    seq = jnp.where(in_p, row0 // lp, bp + (row0 - rows_p) // ls)
    local = jnp.where(in_p, row0 % lp, (row0 - rows_p) % ls)
    length = jnp.where(in_p, lp, ls)
    return seq, local, length


def _mod_spec(layer, which, tm, lay, tile=lambda i: i):
    nseq = lay[0] + lay[2]

    def idx(i):
        seq, _, _ = _locate(tile(i) * tm, lay)
        return ((layer * nseq + seq) * 3 + which, 0, 0)

    return pl.BlockSpec((1, 1, D_MODEL), idx)


def _stream_specs(tm, lay, tile=lambda i: i):
    tiles_p = lay[0] * lay[1] // tm
    tiles_s = lay[2] * lay[3] // tm
    return [
        pl.BlockSpec((tm, D_MODEL), lambda i: (jnp.minimum(tile(i), tiles_p - 1), 0)),
        pl.BlockSpec((tm, D_MODEL), lambda i: (jnp.clip(tile(i) - tiles_p, 0, tiles_s - 1), 0)),
    ]


def _stream_tile(xp_ref, xs_ref, tile, tm, lay):
    return jnp.where(tile * tm < lay[0] * lay[1], xp_ref[...], xs_ref[...])


def _mod_kernel(c_ref, w_ref, b_ref, o_ref):
    c = c_ref[...].astype(BF16)
    w = w_ref[0].astype(BF16)
    o_ref[0] = jnp.dot(c, w, preferred_element_type=F32) + b_ref[0]


def _modulation(c, w_ada, b_ada):
    depth, d, n = w_ada.shape
    nseq = c.shape[0]
    tn = 1536
    return pl.pallas_call(
        _mod_kernel,
        grid=(depth, n // tn),
        in_specs=[
            pl.BlockSpec((nseq, d), lambda l, j: (0, 0)),
            pl.BlockSpec((1, d, tn), lambda l, j: (l, 0, j)),
            pl.BlockSpec((1, 1, tn), lambda l, j: (l, 0, j)),
        ],
        out_specs=pl.BlockSpec((1, nseq, tn), lambda l, j: (l, 0, j)),
        out_shape=jax.ShapeDtypeStruct((depth, nseq, n), F32),
        compiler_params=pltpu.CompilerParams(
            dimension_semantics=("arbitrary", "arbitrary"), vmem_limit_bytes=VMEM_LIMIT_BYTES),
        name="modulation",
    )(c, w_ada, b_ada.reshape(depth, 1, n))


def _rope_table_kernel(cos_ref, sin_ref):
    s = cos_ref.shape[0]
    half = HEAD_DIM // 2
    pos = lax.broadcasted_iota(jnp.int32, (s, HEAD_DIM), 0).astype(F32)
    lane = lax.broadcasted_iota(jnp.int32, (s, HEAD_DIM), 1)
    idx = jnp.where(lane < half, lane, lane - half).astype(F32)
    inv = jnp.exp(idx * (-math.log(ROPE_THETA) / half))
    ang = pos * inv
    cos_ref[...] = jnp.cos(ang)
    sin_ref[...] = jnp.where(lane < half, -jnp.sin(ang), jnp.sin(ang))


def _rope_tables(s):
    return pl.pallas_call(
        _rope_table_kernel,
        out_shape=(jax.ShapeDtypeStruct((s, HEAD_DIM), F32), jax.ShapeDtypeStruct((s, HEAD_DIM), F32)),
        name="rope_tables",
    )()


def _prenorm_kernel(xp_ref, xs_ref, g_ref, shift_ref, scale_ref, h_ref, *, lay):
    tm = h_ref.shape[0]
    x = _stream_tile(xp_ref, xs_ref, pl.program_id(0), tm, lay)
    h_ref[...] = _mod_norm(x, g_ref[...], scale_ref[0], shift_ref[0]).astype(BF16)


def _prenorm(x_streams, gain, mod_rows, lay):
    d = D_MODEL
    t = lay[0] * lay[1] + lay[2] * lay[3]
    tm = NORM_TM
    return pl.pallas_call(
        functools.partial(_prenorm_kernel, lay=lay),
        grid=(t // tm,),
        in_specs=[
            *_stream_specs(tm, lay),
            pl.BlockSpec((1, d), lambda i: (0, 0)),
            _mod_spec(0, 0, tm, lay),
            _mod_spec(0, 1, tm, lay),
        ],
        out_specs=pl.BlockSpec((tm, d), lambda i: (i, 0)),
        out_shape=jax.ShapeDtypeStruct((t, d), BF16),
        compiler_params=pltpu.CompilerParams(
            dimension_semantics=("parallel",), vmem_limit_bytes=VMEM_LIMIT_BYTES),
        name="prenorm",
    )(*x_streams, gain.reshape(1, d), mod_rows, mod_rows)


def _inproj_kernel(h_ref, w_ref, cos_ref, sin_ref, z_ref, qk1_ref, qk2_ref, v1_ref, v2_ref, s_ref):
    j = pl.program_id(1)
    n_chunks, tm, _ = s_ref.shape
    d1, d2 = DILATIONS[1], DILATIONS[2]
    hpg = HEADS_PER_GROUP
    per_piece = INPROJ_PIECE // HEAD_DIM

    def chunk(hh):
        return slice(hh * HEAD_DIM, (hh + 1) * HEAD_DIM)

    def pieces():
        for c in range(n_chunks // per_piece):
            cols = slice(c * INPROJ_PIECE, (c + 1) * INPROJ_PIECE)
            acc = jnp.dot(h_ref[...], w_ref[:, cols], preferred_element_type=F32)
            for k in range(per_piece):
                yield c * per_piece + k, acc[:, chunk(k)]

    def rope(xh):
        return xh * cos_ref[...] + pltpu.roll(xh, HEAD_DIM // 2, 1) * sin_ref[...]

    def to_classes(dst_ref, dil, hh, c_out):
        for r in range(dil):
            dst_ref[r, :, chunk(c_out)] = s_ref[hh, pl.ds(r, tm // dil, stride=dil), :].astype(BF16)

    @pl.when(j == 0)
    def _():
        for hh, xh in pieces():
            z_ref[:, chunk(hh)] = rope(xh).astype(BF16)

    @pl.when(j == 1)
    def _():
        for hh, xh in pieces():
            s_ref[hh] = rope(xh)
            to_classes(qk1_ref, d1, hh, hh)

    @pl.when(j == 2)
    def _():
        for hh, xh in pieces():
            s_ref[hh] = rope(xh)
            to_classes(qk2_ref, d2, hh, hh)

    @pl.when(j == 3)
    def _():
        for hh, xh in pieces():
            s_ref[hh] = xh
            if hh < hpg:
                to_classes(v1_ref, d1, hh, hh)
            else:
                to_classes(v2_ref, d2, hh, hh - hpg)

    @pl.when(j >= N_SPECIAL_TILES)
    def _():
        for hh, xh in pieces():
            z_ref[:, chunk(hh)] = xh.astype(BF16)


def _in_proj(h, w_in, cos, sin, lay):
    t, d = h.shape
    n = w_in.shape[1]
    tm, tn, gw = INPROJ_TM, INPROJ_TN, GROUP_WIDTH
    d1, d2 = DILATIONS[1], DILATIONS[2]

    def rope_idx(i, j):
        _, local, _ = _locate(i * tm, lay)
        return (local // tm, 0)

    def cls_spec(dil, width):
        return pl.BlockSpec((dil, tm // dil, width), lambda i, j: (0, i, 0))

    def cls_shape(dil, width):
        return jax.ShapeDtypeStruct((dil, t // dil, width), BF16)

    return pl.pallas_call(
        _inproj_kernel,
        grid=(t // tm, n // tn),
        in_specs=[
            pl.BlockSpec((tm, d), lambda i, j: (i, 0)),
            pl.BlockSpec((d, tn), lambda i, j: (0, j)),
            pl.BlockSpec((tm, HEAD_DIM), rope_idx),
            pl.BlockSpec((tm, HEAD_DIM), rope_idx),
        ],
        out_specs=[
            pl.BlockSpec((tm, tn), lambda i, j: (i, jnp.maximum(j - (N_SPECIAL_TILES - 1), 0))),
            cls_spec(d1, 2 * gw), cls_spec(d2, 2 * gw), cls_spec(d1, gw), cls_spec(d2, gw),
        ],
        out_shape=[
            jax.ShapeDtypeStruct((t, Z_WIDTH), BF16),
            cls_shape(d1, 2 * gw), cls_shape(d2, 2 * gw), cls_shape(d1, gw), cls_shape(d2, gw),
        ],
        scratch_shapes=[pltpu.VMEM((tn // HEAD_DIM, tm, HEAD_DIM), F32)],
        compiler_params=pltpu.CompilerParams(
            dimension_semantics=("parallel", "arbitrary"), vmem_limit_bytes=VMEM_LIMIT_BYTES),
        name="in_proj",
    )(h, w_in, cos, sin)


def _attn_kernel(q_ref, kp_ref, km_ref, kn_ref, vp_ref, vm_ref, vn_ref, o_ref, lse_ref, k_buf, v_buf, *, dil, lay):
    bq = q_ref.shape[0]
    r = BAND_RADIUS
    k_buf[0:r] = kp_ref[...]
    k_buf[r:r + bq] = km_ref[...]
    k_buf[r + bq:] = kn_ref[...]
    v_buf[0:r] = vp_ref[...]
    v_buf[r:r + bq] = vm_ref[...]
    v_buf[r + bq:] = vn_ref[...]

    nk = ATTN_SUB + 2 * r
    qi = lax.broadcasted_iota(jnp.int32, (ATTN_SUB, nk), 0)
    kj = lax.broadcasted_iota(jnp.int32, (ATTN_SUB, nk), 1) - r
    in_band = jnp.abs(kj - qi) <= r
    lane = lax.broadcasted_iota(jnp.int32, (ATTN_SUB, HEAD_DIM), 1)
    inv_sqrt = 1.0 / math.sqrt(HEAD_DIM)

    for sb in range(bq // ATTN_SUB):
        a = sb * ATTN_SUB
        _, local, length = _locate(pl.program_id(1) * bq + a, lay, dil)
        mask = in_band & (kj + local >= 0) & (kj + local < length)
        lse_tile = jnp.zeros((ATTN_SUB, HEAD_DIM), F32)
        for hh in range(HEADS_PER_GROUP):
            cs = slice(hh * HEAD_DIM, (hh + 1) * HEAD_DIM)
            q = q_ref[a:a + ATTN_SUB, cs]
            k = k_buf[a:a + nk, cs]
            v = v_buf[a:a + nk, cs]
            s = lax.dot_general(q, k, (((1,), (1,)), ((), ())), preferred_element_type=F32) * inv_sqrt
            s = jnp.where(mask, s, NEG_BIG)
            m = jnp.max(s, axis=-1, keepdims=True)
            p = jnp.exp(s - m)
            l = jnp.sum(p, axis=-1, keepdims=True)
            o = jnp.dot(p.astype(BF16), v, preferred_element_type=F32) * (1.0 / l)
            o_ref[a:a + ATTN_SUB, cs] = o.astype(BF16)
            lse_tile = jnp.where(lane == hh, m + jnp.log(l), lse_tile)
        lse_ref[a:a + ATTN_SUB, :] = lse_tile


def _band_attention(qk, v, q_col, k_col, v_col, dil, lay):
    rows = qk.shape[1]
    bq, r, gw = ATTN_BQ, BAND_RADIUS, GROUP_WIDTH
    assert rows % bq == 0 and bq % r == 0
    halo_last = rows // r - 1

    def main(col):
        return pl.BlockSpec((None, bq, gw), lambda c, n: (c, n, col))

    def prev(col):
        return pl.BlockSpec((None, r, gw), lambda c, n: (c, jnp.maximum(n * (bq // r) - 1, 0), col))

    def nxt(col):
        return pl.BlockSpec((None, r, gw), lambda c, n: (c, jnp.minimum((n + 1) * (bq // r), halo_last), col))

    return pl.pallas_call(
        functools.partial(_attn_kernel, dil=dil, lay=lay),
        grid=(dil, rows // bq),
        in_specs=[main(q_col), prev(k_col), main(k_col), nxt(k_col), prev(v_col), main(v_col), nxt(v_col)],
        out_specs=[
            pl.BlockSpec((None, bq, gw), lambda c, n: (c, n, 0)),
            pl.BlockSpec((None, bq, HEAD_DIM), lambda c, n: (c, n, 0)),
        ],
        out_shape=[
            jax.ShapeDtypeStruct((dil, rows, gw), BF16),
            jax.ShapeDtypeStruct((dil, rows, HEAD_DIM), F32),
        ],
        scratch_shapes=[pltpu.VMEM((bq + 2 * r, gw), BF16), pltpu.VMEM((bq + 2 * r, gw), BF16)],
        compiler_params=pltpu.CompilerParams(
            dimension_semantics=("parallel", "parallel"), vmem_limit_bytes=VMEM_LIMIT_BYTES),
        name=f"band_attn_d{dil}",
    )(qk, qk, qk, qk, v, v, v)


def _merge_kernel(*refs, lay, block_off, n_tiles, final, split_x):
    n_x = 2 if split_x else 1
    (pin_ref, pprev_ref, pnext_ref, pgate_ref, ag0_ref, ag1_ref, ag2_ref,
     o0_ref, o1_ref, o2_ref, l0_ref, l1_ref, l2_ref, gpool_ref, gattn_ref) = refs[:15]
    x_refs = refs[15:15 + n_x]
    (gate_ref, wgrp_ref, pscale_ref, wpp_ref, wpa_ref, wout_ref,
     ngain_ref, nshift_ref, nscale_ref) = refs[15 + n_x:24 + n_x]
    outs_and_scratch = refs[24 + n_x:]
    if final:
        y_ref, band_ref, ext_ref, a0_ref, b0_ref, a1_ref, b1_ref, m_ref, oi_ref, li_ref, wt_ref, sq_ref = outs_and_scratch
    else:
        out_ref, h_ref, band_ref, ext_ref, a0_ref, b0_ref, a1_ref, b1_ref, m_ref, oi_ref, li_ref, wt_ref, sq_ref = outs_and_scratch
    tm = gpool_ref.shape[0]
    hl = POOL_HALO
    gw = GROUP_WIDTH
    hpg = HEADS_PER_GROUP
    s = pl.program_id(0)

    @pl.when(s == 0)
    def _():
        i = lax.broadcasted_iota(jnp.int32, (tm, tm + 2 * hl), 0)
        e = lax.broadcasted_iota(jnp.int32, (tm, tm + 2 * hl), 1)
        off = e - hl - i
        for g, w in enumerate(POOL_WINDOWS):
            band_ref[g] = jnp.where((off >= -(w // 2)) & (off < w // 2), 1.0, 0.0).astype(BF16)
        a0_ref[...] = jnp.zeros_like(a0_ref)
        b0_ref[...] = jnp.zeros_like(b0_ref)

    def step(a_cur, b_cur, a_nxt, b_nxt):
        back_tile = jnp.maximum(s - 1, 0) + block_off
        front_tile = jnp.minimum(s, n_tiles - 1) + block_off
        _, t0, length = _locate(front_tile * tm, lay)
        n_pieces = D_MODEL // MERGE_PIECE

        def piece(c):
            return slice(c * MERGE_PIECE, (c + 1) * MERGE_PIECE)

        def head(hh):
            return slice(hh * HEAD_DIM, (hh + 1) * HEAD_DIM)

        def front_prepare():
            zeros = jnp.zeros((hl, POOL_WIDTH), BF16)
            ext_ref[0:hl] = jnp.where(t0 == 0, zeros, pprev_ref[...])
            ext_ref[hl:hl + tm] = pin_ref[...]
            ext_ref[hl + tm:] = jnp.where(t0 + tm == length, zeros, pnext_ref[...])
            for slot, (o_ref, l_ref, dil) in enumerate(((o1_ref, l1_ref, DILATIONS[1]), (o2_ref, l2_ref, DILATIONS[2]))):
                for r in range(dil):
                    rows = pl.ds(r, tm // dil, stride=dil)
                    for hh in range(hpg):
                        oi_ref[slot * hpg + hh, rows, :] = o_ref[r, :, head(hh)].astype(F32)
                    li_ref[slot, rows, :] = l_ref[r]
            lses = (l0_ref[...], li_ref[0], li_ref[1])
            mx = jnp.maximum(jnp.maximum(lses[0], lses[1]), lses[2])
            es = [jnp.exp(l - mx) for l in lses]
            inv_den = 1.0 / (es[0] + es[1] + es[2])
            for g in range(len(DILATIONS)):
                wt_ref[g] = es[g] * inv_den

        def front_head(g, hh):
            def run():
                ag_ref = (ag0_ref, ag1_ref, ag2_ref)[g]
                o = o0_ref[:, head(hh)].astype(F32) if g == 0 else oi_ref[(g - 1) * hpg + hh]
                gate_half = ag_ref[:, head(hh)].astype(F32)
                b = o * wt_ref[g, :, hh:hh + 1] * (gate_half * _one_plus_tanh(gate_half))
                b_nxt[:, g * gw + hh * HEAD_DIM:g * gw + (hh + 1) * HEAD_DIM] = b.astype(BF16)
            return run

        def front_pool(g):
            def run():
                w = POOL_WINDOWS[g]
                cs = slice(g * POOL_GROUP, (g + 1) * POOL_GROUP)
                t = t0 + lax.broadcasted_iota(jnp.int32, (tm, 1), 0)
                acc = jnp.dot(band_ref[g], ext_ref[:, cs], preferred_element_type=F32)
                cnt = (jnp.minimum(t + w // 2, length) - jnp.maximum(t - w // 2, 0)).astype(F32)
                p = acc * (1.0 / cnt) - pin_ref[:, cs].astype(F32)
                y = jnp.dot(p.astype(BF16), wgrp_ref[g], preferred_element_type=F32)
                gate_half = pgate_ref[:, cs].astype(F32)
                a = y * pscale_ref[:, cs] * (gate_half * _one_plus_tanh(gate_half))
                a_nxt[:, cs] = a.astype(BF16)
            return run

        front = [front_head(g, hh) for g in range(len(DILATIONS)) for hh in range(hpg)]
        front += [front_pool(g) for g in range(len(POOL_WINDOWS))]

        x_dst = y_ref if final else out_ref

        def back_merge(c):
            def run():
                pp = jnp.dot(a_cur[...], wpp_ref[:, piece(c)], preferred_element_type=F32)
                pa = jnp.dot(b_cur[...], wpa_ref[:, piece(c)], preferred_element_type=F32)
                merged = (_one_plus_tanh(gpool_ref[:, piece(c)].astype(F32)) * pp
                          + _one_plus_tanh(gattn_ref[:, piece(c)].astype(F32)) * pa)
                m_ref[:, piece(c)] = merged.astype(BF16)
            return run

        def back_out(c):
            def run():
                out = jnp.dot(m_ref[...], wout_ref[:, piece(c)], preferred_element_type=F32)
                if split_x:
                    x = jnp.where(back_tile * tm < lay[0] * lay[1], x_refs[0][:, piece(c)], x_refs[1][:, piece(c)])
                else:
                    x = x_refs[0][:, piece(c)]
                x_new = x + gate_ref[0, :, piece(c)] * out
                x_dst[:, piece(c)] = x_new
                sq_ref[c] = jnp.sum(x_new * x_new, axis=-1, keepdims=True)
            return run

        back = [back_merge(c) for c in range(n_pieces)] + [back_out(c) for c in range(n_pieces)]

        front_prepare()
        for k in range(max(len(back), len(front))):
            if k < len(back):
                back[k]()
            if k < len(front):
                front[k]()

        sq = sq_ref[0]
        for c in range(1, n_pieces):
            sq = sq + sq_ref[c]
        inv_rms = lax.rsqrt(sq * (1.0 / D_MODEL) + NORM_EPS)
        for c in range(n_pieces):
            xn = x_dst[:, piece(c)] * inv_rms * ngain_ref[:, piece(c)]
            if final:
                y_ref[:, piece(c)] = xn
            else:
                h_ref[:, piece(c)] = (xn * (1.0 + nscale_ref[0, :, piece(c)]) + nshift_ref[0, :, piece(c)]).astype(BF16)

    @pl.when(s % 2 == 0)
    def _():
        step(a0_ref, b0_ref, a1_ref, b1_ref)

    @pl.when(s % 2 == 1)
    def _():
        step(a1_ref, b1_ref, a0_ref, b0_ref)


def _merge(z, attn, x, mod_rows, layer, w_grp, pool_scale, w_pp, w_pa, w_out, next_gain, lay, *,
           final=False, row_start=0, rows=None):
    split_x = isinstance(x, (tuple, list))
    d = D_MODEL
    t = z.shape[0]
    rows = t if rows is None else rows
    tm = MERGE_TM
    hb = POOL_HALO
    assert row_start % tm == 0 and rows % tm == 0
    off = row_start // tm
    n_tiles = rows // tm
    halo_last = t // hb - 1
    (o0, l0), (o1, l1), (o2, l2) = attn
    d1, d2 = DILATIONS[1], DILATIONS[2]

    def front(s):
        return jnp.minimum(s, n_tiles - 1) + off

    def back(s):
        return jnp.maximum(s - 1, 0) + off

    def zcols(tile, width, col):
        assert col % width == 0
        return pl.BlockSpec((tm, width), lambda s: (tile(s), col // width))

    def const(shape):
        return pl.BlockSpec(shape, lambda s: (0,) * len(shape), pipeline_mode=pl.Buffered(1))

    def row(tile, width):
        return pl.BlockSpec((tm, width), lambda s: (tile(s), 0))

    def cls(dil, width):
        return pl.BlockSpec((dil, tm // dil, width), lambda s: (0, front(s), 0))

    next_layer = layer if final else layer + 1
    out_row = pl.BlockSpec((tm, d), lambda s: (jnp.maximum(s - 1, 0), 0))
    if final:
        out_specs = out_row
        out_shape = jax.ShapeDtypeStruct((rows, d), F32)
    else:
        out_specs = [out_row, out_row]
        out_shape = [jax.ShapeDtypeStruct((rows, d), F32), jax.ShapeDtypeStruct((rows, d), BF16)]
    x_specs = _stream_specs(tm, lay, back) if split_x else [row(back, d)]
    x_args = tuple(x) if split_x else (x,)

    return pl.pallas_call(
        functools.partial(_merge_kernel, lay=lay, block_off=off, n_tiles=n_tiles, final=final, split_x=split_x),
        grid=(n_tiles + 1,),
        in_specs=[
            zcols(front, POOL_WIDTH, ZC_POOL_IN),
            pl.BlockSpec((hb, POOL_WIDTH),
                         lambda s: (jnp.maximum(front(s) * (tm // hb) - 1, 0), ZC_POOL_IN // POOL_WIDTH)),
            pl.BlockSpec((hb, POOL_WIDTH),
                         lambda s: (jnp.minimum((front(s) + 1) * (tm // hb), halo_last), ZC_POOL_IN // POOL_WIDTH)),
            zcols(front, POOL_WIDTH, ZC_POOL_GATE),
            zcols(front, GROUP_WIDTH, ZC_AG0),
            zcols(front, GROUP_WIDTH, ZC_AG1),
            zcols(front, GROUP_WIDTH, ZC_AG2),
            row(front, GROUP_WIDTH), cls(d1, GROUP_WIDTH), cls(d2, GROUP_WIDTH),
            row(front, HEAD_DIM), cls(d1, HEAD_DIM), cls(d2, HEAD_DIM),
            zcols(back, D_MODEL, ZC_G_POOL),
            zcols(back, D_MODEL, ZC_G_ATTN),
            *x_specs,
            _mod_spec(layer, 2, tm, lay, back),
            const(w_grp.shape), const((1, POOL_WIDTH)), const(w_pp.shape), const(w_pa.shape), const(w_out.shape),
            const((1, d)),
            _mod_spec(next_layer, 0, tm, lay, back),
            _mod_spec(next_layer, 1, tm, lay, back),
        ],
        out_specs=out_specs,
        out_shape=out_shape,
        scratch_shapes=[
            pltpu.VMEM((len(POOL_WINDOWS), tm, tm + 2 * POOL_HALO), BF16),
            pltpu.VMEM((tm + 2 * POOL_HALO, POOL_WIDTH), BF16),
            pltpu.VMEM((tm, POOL_WIDTH), BF16),
            pltpu.VMEM((tm, ATTN_WIDTH), BF16),
            pltpu.VMEM((tm, POOL_WIDTH), BF16),
            pltpu.VMEM((tm, ATTN_WIDTH), BF16),
            pltpu.VMEM((tm, D_MODEL), BF16),
            pltpu.VMEM((2 * HEADS_PER_GROUP, tm, HEAD_DIM), F32),
            pltpu.VMEM((2, tm, HEAD_DIM), F32),
            pltpu.VMEM((len(DILATIONS), tm, HEAD_DIM), F32),
            pltpu.VMEM((D_MODEL // MERGE_PIECE, tm, 1), F32),
        ],
        compiler_params=pltpu.CompilerParams(
            dimension_semantics=("arbitrary",), vmem_limit_bytes=VMEM_LIMIT_BYTES),
        name="merge_final" if final else "merge",
    )(z, z, z, z, z, z, z, o0, o1, o2, l0, l1, l2, z, z, *x_args, mod_rows,
      w_grp, pool_scale.reshape(1, POOL_WIDTH), w_pp, w_pa, w_out, next_gain.reshape(1, d), mod_rows, mod_rows)


def _prepare_in_proj_weight(w):
    gw = GROUP_WIDTH
    chunks = [w[:, c * gw:(c + 1) * gw] * (0.5 if c in HALVED_CHUNKS else 1.0) for c in IN_PROJ_CHUNK_ORDER]
    return jnp.concatenate(chunks, axis=1).astype(BF16)


def kernel(x_prompt, x_sample, c_prompt, c_sample, norm_gain, w_ada, b_ada, w_in, w_pool_grp, pool_scale,
           w_proj_pool, w_proj_attn, w_out, final_gain):
    bp, sp, d = x_prompt.shape
    bs, ss, _ = x_sample.shape
    assert d == D_MODEL
    lay = (bp, sp, bs, ss)
    depth = w_in.shape[0]
    nseq = bp + bs
    gw = GROUP_WIDTH
    for s in (sp, ss):
        assert s % INPROJ_TM == 0 and s % MERGE_TM == 0 and s % (max(DILATIONS) * ATTN_SUB) == 0

    x = (x_prompt.reshape(bp * sp, d), x_sample.reshape(bs * ss, d))
    c = jnp.concatenate([c_prompt, c_sample], axis=0)
    mod_rows = _modulation(c, w_ada, b_ada).reshape(depth * nseq * 3, 1, d)
    cos, sin = _rope_tables(max(sp, ss))

    h = _prenorm(x, norm_gain[0], mod_rows, lay)
    for layer in range(depth):
        z, qk1, qk2, v1, v2 = _in_proj(h, _prepare_in_proj_weight(w_in[layer]), cos, sin, lay)
        o0, l0 = _band_attention(z[None], z[None], ZC_Q0 // gw, ZC_K0 // gw, ZC_V0 // gw, DILATIONS[0], lay)
        attn = [
            (o0[0], l0[0]),
            _band_attention(qk1, v1, 0, 1, 0, DILATIONS[1], lay),
            _band_attention(qk2, v2, 0, 1, 0, DILATIONS[2], lay),
        ]
        weights = (w_pool_grp[layer].astype(BF16), pool_scale[layer], (0.5 * w_proj_pool[layer]).astype(BF16),
                   (0.5 * w_proj_attn[layer]).astype(BF16), w_out[layer].astype(BF16))
        if layer + 1 < depth:
            x, h = _merge(z, attn, x, mod_rows, layer, *weights, norm_gain[layer + 1], lay)
        else:
            y_prompt = _merge(z, attn, x, mod_rows, layer, *weights, final_gain, lay,
                              final=True, row_start=0, rows=bp * sp)
            y_sample = _merge(z, attn, x, mod_rows, layer, *weights, final_gain, lay,
                              final=True, row_start=bp * sp, rows=bs * ss)
    return (y_prompt.reshape(bp, sp, d), y_sample.reshape(bs, ss, d))
```

```python
import functools
import math

import jax
import jax.numpy as jnp
from jax import lax
from jax.experimental import pallas as pl
from jax.experimental.pallas import tpu as pltpu

F32 = jnp.float32
BF16 = jnp.bfloat16

D_MODEL = 2048
POOL_WIDTH = D_MODEL // 2
POOL_WINDOWS = (2, 4, 8, 16)
POOL_GROUP = POOL_WIDTH // len(POOL_WINDOWS)
POOL_HALO = 64
assert POOL_HALO >= max(POOL_WINDOWS) // 2
HEAD_DIM = 128
ATTN_GROUPS = ((128, 1), (512, 4), (2048, 16))
DILATIONS = tuple(dil for _, dil in ATTN_GROUPS)
HEADS_PER_GROUP = 4
GROUP_WIDTH = HEADS_PER_GROUP * HEAD_DIM
ATTN_WIDTH = GROUP_WIDTH * len(ATTN_GROUPS)
BAND_RADIUS = 64
assert all(win // (2 * dil) == BAND_RADIUS for win, dil in ATTN_GROUPS)
assert DILATIONS[0] == 1 and len(DILATIONS) == 3
ROPE_THETA = 10000.0
NORM_EPS = 1e-6
NEG_BIG = -1e30
IN_WIDTH = 2 * POOL_WIDTH + 4 * ATTN_WIDTH + 2 * D_MODEL

_CH_POOL_IN, _CH_POOL_GATE, _CH_Q, _CH_K, _CH_V, _CH_AG, _CH_GP, _CH_GA = 0, 2, 4, 7, 10, 13, 16, 20
IN_PROJ_CHUNK_ORDER = (
    _CH_Q, _CH_K, _CH_Q + 1, _CH_K + 1, _CH_Q + 2, _CH_K + 2, _CH_V + 1, _CH_V + 2,
    _CH_V, _CH_AG, *range(_CH_GP, _CH_GP + 4), *range(_CH_GA, _CH_GA + 4),
    _CH_POOL_IN, _CH_POOL_IN + 1, _CH_POOL_GATE, _CH_POOL_GATE + 1, _CH_AG + 1, _CH_AG + 2)
assert sorted(IN_PROJ_CHUNK_ORDER) == list(range(IN_WIDTH // GROUP_WIDTH))
HALVED_CHUNKS = frozenset((*range(_CH_POOL_GATE, _CH_POOL_GATE + 2), *range(_CH_AG, _CH_AG + 3),
                           *range(_CH_GP, _CH_GP + 4), *range(_CH_GA, _CH_GA + 4)))
N_SPECIAL_TILES = 4
Z_WIDTH = IN_WIDTH - 6 * GROUP_WIDTH
ZC_Q0, ZC_K0, ZC_V0, ZC_AG0 = 0, GROUP_WIDTH, 2 * GROUP_WIDTH, 3 * GROUP_WIDTH
ZC_G_POOL = 4 * GROUP_WIDTH
ZC_G_ATTN = ZC_G_POOL + D_MODEL
ZC_POOL_IN = ZC_G_ATTN + D_MODEL
ZC_POOL_GATE = ZC_POOL_IN + POOL_WIDTH
ZC_AG1 = ZC_POOL_GATE + POOL_WIDTH
ZC_AG2 = ZC_AG1 + GROUP_WIDTH

VMEM_LIMIT_BYTES = 56 * 1024 * 1024

INPROJ_TM = 1024
INPROJ_TN = 2 * GROUP_WIDTH
INPROJ_PIECE = 256
ATTN_BQ = 512
ATTN_SUB = 128
MERGE_TM = 256
MERGE_PIECE = 256
NORM_TM = 512


def _one_plus_tanh(x_half):
    return 1.0 + jnp.tanh(x_half)


def _mod_norm(x, gain, scale, shift):
    xn = x * lax.rsqrt(jnp.mean(x * x, axis=-1, keepdims=True) + NORM_EPS) * gain
    return xn * (1.0 + scale) + shift


def _locate(row0, lay, dil=1):
    bp, sp, bs, ss = lay
    rows_p, lp, ls = bp * sp // dil, sp // dil, ss // dil
    in_p = row0 < rows_p
    seq = jnp.where(in_p, row0 // lp, bp + (row0 - rows_p) // ls)
    local = jnp.where(in_p, row0 % lp, (row0 - rows_p) % ls)
    length = jnp.where(in_p, lp, ls)
    return seq, local, length


def _mod_spec(layer, which, tm, lay, tile=lambda i: i):
    nseq = lay[0] + lay[2]

    def idx(i):
        seq, _, _ = _locate(tile(i) * tm, lay)
        return ((layer * nseq + seq) * 3 + which, 0, 0)

    return pl.BlockSpec((1, 1, D_MODEL), idx)


def _stream_specs(tm, lay, tile=lambda i: i):
    tiles_p = lay[0] * lay[1] // tm
    tiles_s = lay[2] * lay[3] // tm
    return [
        pl.BlockSpec((tm, D_MODEL), lambda i: (jnp.minimum(tile(i), tiles_p - 1), 0)),
        pl.BlockSpec((tm, D_MODEL), lambda i: (jnp.clip(tile(i) - tiles_p, 0, tiles_s - 1), 0)),
    ]


def _stream_tile(xp_ref, xs_ref, tile, tm, lay):
    return jnp.where(tile * tm < lay[0] * lay[1], xp_ref[...], xs_ref[...])


def _mod_kernel(c_ref, w_ref, b_ref, o_ref):
    c = c_ref[...].astype(BF16)
    w = w_ref[0].astype(BF16)
    o_ref[0] = jnp.dot(c, w, preferred_element_type=F32) + b_ref[0]


def _modulation(c, w_ada, b_ada):
    depth, d, n = w_ada.shape
    nseq = c.shape[0]
    tn = 1536
    return pl.pallas_call(
        _mod_kernel,
        grid=(depth, n // tn),
        in_specs=[
            pl.BlockSpec((nseq, d), lambda l, j: (0, 0)),
            pl.BlockSpec((1, d, tn), lambda l, j: (l, 0, j)),
            pl.BlockSpec((1, 1, tn), lambda l, j: (l, 0, j)),
        ],
        out_specs=pl.BlockSpec((1, nseq, tn), lambda l, j: (l, 0, j)),
        out_shape=jax.ShapeDtypeStruct((depth, nseq, n), F32),
        compiler_params=pltpu.CompilerParams(
            dimension_semantics=("arbitrary", "arbitrary"), vmem_limit_bytes=VMEM_LIMIT_BYTES),
        name="modulation",
    )(c, w_ada, b_ada.reshape(depth, 1, n))


def _rope_table_kernel(cos_ref, sin_ref):
    s = cos_ref.shape[0]
    half = HEAD_DIM // 2
    pos = lax.broadcasted_iota(jnp.int32, (s, HEAD_DIM), 0).astype(F32)
    lane = lax.broadcasted_iota(jnp.int32, (s, HEAD_DIM), 1)
    idx = jnp.where(lane < half, lane, lane - half).astype(F32)
    inv = jnp.exp(idx * (-math.log(ROPE_THETA) / half))
    ang = pos * inv
    cos_ref[...] = jnp.cos(ang)
    sin_ref[...] = jnp.where(lane < half, -jnp.sin(ang), jnp.sin(ang))


def _rope_tables(s):
    return pl.pallas_call(
        _rope_table_kernel,
        out_shape=(jax.ShapeDtypeStruct((s, HEAD_DIM), F32), jax.ShapeDtypeStruct((s, HEAD_DIM), F32)),
        name="rope_tables",
    )()


def _prenorm_kernel(xp_ref, xs_ref, g_ref, shift_ref, scale_ref, h_ref, *, lay):
    tm = h_ref.shape[0]
    x = _stream_tile(xp_ref, xs_ref, pl.program_id(0), tm, lay)
    h_ref[...] = _mod_norm(x, g_ref[...], scale_ref[0], shift_ref[0]).astype(BF16)


def _prenorm(x_streams, gain, mod_rows, lay):
    d = D_MODEL
    t = lay[0] * lay[1] + lay[2] * lay[3]
    tm = NORM_TM
    return pl.pallas_call(
        functools.partial(_prenorm_kernel, lay=lay),
        grid=(t // tm,),
        in_specs=[
            *_stream_specs(tm, lay),
            pl.BlockSpec((1, d), lambda i: (0, 0)),
            _mod_spec(0, 0, tm, lay),
            _mod_spec(0, 1, tm, lay),
        ],
        out_specs=pl.BlockSpec((tm, d), lambda i: (i, 0)),
        out_shape=jax.ShapeDtypeStruct((t, d), BF16),
        compiler_params=pltpu.CompilerParams(
            dimension_semantics=("parallel",), vmem_limit_bytes=VMEM_LIMIT_BYTES),
        name="prenorm",
    )(*x_streams, gain.reshape(1, d), mod_rows, mod_rows)


def _inproj_kernel(h_ref, wa_ref, wb_ref, cos_ref, sin_ref, z_ref, qk1_ref, qk2_ref, v1_ref, v2_ref, s_ref, t_ref):
    j = pl.program_id(1)
    n_chunks, tm, _ = s_ref.shape
    d1, d2 = DILATIONS[1], DILATIONS[2]
    hpg = HEADS_PER_GROUP
    per_piece = INPROJ_PIECE // HEAD_DIM

    def chunk(hh):
        return slice(hh * HEAD_DIM, (hh + 1) * HEAD_DIM)

    def pieces():
        pieces_per_half = GROUP_WIDTH // INPROJ_PIECE
        for c in range(n_chunks // per_piece):
            w_ref = wa_ref if c < pieces_per_half else wb_ref
            c_in = c % pieces_per_half
            cols = slice(c_in * INPROJ_PIECE, (c_in + 1) * INPROJ_PIECE)
            acc = jnp.dot(h_ref[...], w_ref[:, cols], preferred_element_type=F32)
            for k in range(per_piece):
                yield c * per_piece + k, acc[:, chunk(k)]

    def rope(xh):
        return xh * cos_ref[...] + pltpu.roll(xh, HEAD_DIM // 2, 1) * sin_ref[...]

    def to_classes(dst_ref, dil, hh, c_out):
        if dil == d1 * d1:
            for r1 in range(d1):
                t_ref[hh % 2, r1] = s_ref[hh, pl.ds(r1, tm // d1, stride=d1), :]
            for r in range(dil):
                rows = pl.ds(r // d1, tm // dil, stride=d1)
                dst_ref[r, :, chunk(c_out)] = t_ref[hh % 2, r % d1, rows, :].astype(BF16)
        else:
            for r in range(dil):
                dst_ref[r, :, chunk(c_out)] = s_ref[hh, pl.ds(r, tm // dil, stride=dil), :].astype(BF16)

    @pl.when(j == 0)
    def _():
        for hh, xh in pieces():
            z_ref[:, chunk(hh)] = rope(xh).astype(BF16)

    @pl.when(j == 1)
    def _():
        for hh, xh in pieces():
            s_ref[hh] = rope(xh)
            to_classes(qk1_ref, d1, hh, hh)

    @pl.when(j == 2)
    def _():
        for hh, xh in pieces():
            s_ref[hh] = rope(xh)
            to_classes(qk2_ref, d2, hh, hh)

    @pl.when(j == 3)
    def _():
        for hh, xh in pieces():
            s_ref[hh] = xh
            if hh < hpg:
                to_classes(v1_ref, d1, hh, hh)
            else:
                to_classes(v2_ref, d2, hh, hh - hpg)

    @pl.when(j >= N_SPECIAL_TILES)
    def _():
        for hh, xh in pieces():
            z_ref[:, chunk(hh)] = xh.astype(BF16)


def _in_proj(h, w_in, layer, cos, sin, lay):
    t, d = h.shape
    n = w_in.shape[2]
    tm, tn, gw = INPROJ_TM, INPROJ_TN, GROUP_WIDTH
    d1, d2 = DILATIONS[1], DILATIONS[2]

    def rope_idx(i, j):
        _, local, _ = _locate(i * tm, lay)
        return (local // tm, 0)

    def cls_spec(dil, width):
        return pl.BlockSpec((dil, tm // dil, width), lambda i, j: (0, i, 0))

    def cls_shape(dil, width):
        return jax.ShapeDtypeStruct((dil, t // dil, width), BF16)

    def w_half_spec(half):
        chunk_ids = IN_PROJ_CHUNK_ORDER[half::2]

        def idx(i, j):
            chunk = chunk_ids[0]
            for k in range(1, len(chunk_ids)):
                chunk = jnp.where(j == k, chunk_ids[k], chunk)
            return (layer, 0, chunk)

        return pl.BlockSpec((None, d, gw), idx)

    return pl.pallas_call(
        _inproj_kernel,
        grid=(t // tm, n // tn),
        in_specs=[
            pl.BlockSpec((tm, d), lambda i, j: (i, 0)),
            w_half_spec(0),
            w_half_spec(1),
            pl.BlockSpec((tm, HEAD_DIM), rope_idx),
            pl.BlockSpec((tm, HEAD_DIM), rope_idx),
        ],
        out_specs=[
            pl.BlockSpec((tm, tn), lambda i, j: (i, jnp.maximum(j - (N_SPECIAL_TILES - 1), 0))),
            cls_spec(d1, 2 * gw), cls_spec(d2, 2 * gw), cls_spec(d1, gw), cls_spec(d2, gw),
        ],
        out_shape=[
            jax.ShapeDtypeStruct((t, Z_WIDTH), BF16),
            cls_shape(d1, 2 * gw), cls_shape(d2, 2 * gw), cls_shape(d1, gw), cls_shape(d2, gw),
        ],
        scratch_shapes=[pltpu.VMEM((tn // HEAD_DIM, tm, HEAD_DIM), F32),
                        pltpu.VMEM((2, d1, tm // d1, HEAD_DIM), F32)],
        compiler_params=pltpu.CompilerParams(
            dimension_semantics=("parallel", "arbitrary"), vmem_limit_bytes=VMEM_LIMIT_BYTES),
        name="in_proj",
    )(h, w_in, w_in, cos, sin)


def _attn_kernel(q_ref, kp_ref, km_ref, kn_ref, vp_ref, vm_ref, vn_ref, o_ref, lse_ref, k_buf, v_buf, *, dil, lay):
    bq = q_ref.shape[0]
    r = BAND_RADIUS
    k_buf[0:r] = kp_ref[...]
    k_buf[r:r + bq] = km_ref[...]
    k_buf[r + bq:] = kn_ref[...]
    v_buf[0:r] = vp_ref[...]
    v_buf[r:r + bq] = vm_ref[...]
    v_buf[r + bq:] = vn_ref[...]

    nk = ATTN_SUB + 2 * r
    qi = lax.broadcasted_iota(jnp.int32, (ATTN_SUB, nk), 0)
    kj = lax.broadcasted_iota(jnp.int32, (ATTN_SUB, nk), 1) - r
    in_band = jnp.abs(kj - qi) <= r
    lane = lax.broadcasted_iota(jnp.int32, (ATTN_SUB, HEAD_DIM), 1)
    inv_sqrt = 1.0 / math.sqrt(HEAD_DIM)

    for sb in range(bq // ATTN_SUB):
        a = sb * ATTN_SUB
        _, local, length = _locate(pl.program_id(1) * bq + a, lay, dil)
        mask = in_band & (kj + local >= 0) & (kj + local < length)
        lse_tile = jnp.zeros((ATTN_SUB, HEAD_DIM), F32)
        for hh in range(HEADS_PER_GROUP):
            cs = slice(hh * HEAD_DIM, (hh + 1) * HEAD_DIM)
            q = q_ref[a:a + ATTN_SUB, cs]
            k = k_buf[a:a + nk, cs]
            v = v_buf[a:a + nk, cs]
            s = lax.dot_general(q, k, (((1,), (1,)), ((), ())), preferred_element_type=F32) * inv_sqrt
            s = jnp.where(mask, s, NEG_BIG)
            m = jnp.max(s, axis=-1, keepdims=True)
            p = jnp.exp(s - m)
            l = jnp.sum(p, axis=-1, keepdims=True)
            o = jnp.dot(p.astype(BF16), v, preferred_element_type=F32) * (1.0 / l)
            o_ref[a:a + ATTN_SUB, cs] = o.astype(BF16)
            lse_tile = jnp.where(lane == hh, m + jnp.log(l), lse_tile)
        lse_ref[a:a + ATTN_SUB, :] = lse_tile


def _band_attention(qk, v, q_col, k_col, v_col, dil, lay):
    rows = qk.shape[1]
    bq, r, gw = ATTN_BQ, BAND_RADIUS, GROUP_WIDTH
    assert rows % bq == 0 and bq % r == 0
    halo_last = rows // r - 1

    def main(col):
        return pl.BlockSpec((None, bq, gw), lambda c, n: (c, n, col))

    def prev(col):
        return pl.BlockSpec((None, r, gw), lambda c, n: (c, jnp.maximum(n * (bq // r) - 1, 0), col))

    def nxt(col):
        return pl.BlockSpec((None, r, gw), lambda c, n: (c, jnp.minimum((n + 1) * (bq // r), halo_last), col))

    return pl.pallas_call(
        functools.partial(_attn_kernel, dil=dil, lay=lay),
        grid=(dil, rows // bq),
        in_specs=[main(q_col), prev(k_col), main(k_col), nxt(k_col), prev(v_col), main(v_col), nxt(v_col)],
        out_specs=[
            pl.BlockSpec((None, bq, gw), lambda c, n: (c, n, 0)),
            pl.BlockSpec((None, bq, HEAD_DIM), lambda c, n: (c, n, 0)),
        ],
        out_shape=[
            jax.ShapeDtypeStruct((dil, rows, gw), BF16),
            jax.ShapeDtypeStruct((dil, rows, HEAD_DIM), F32),
        ],
        scratch_shapes=[pltpu.VMEM((bq + 2 * r, gw), BF16), pltpu.VMEM((bq + 2 * r, gw), BF16)],
        compiler_params=pltpu.CompilerParams(
            dimension_semantics=("parallel", "parallel"), vmem_limit_bytes=VMEM_LIMIT_BYTES),
        name=f"band_attn_d{dil}",
    )(qk, qk, qk, qk, v, v, v)


def _merge_kernel(*refs, lay, block_off, n_tiles, final, split_x):
    n_x = 2 if split_x else 1
    (pin_ref, pprev_ref, pnext_ref, pgate_ref, ag0_ref, ag1_ref, ag2_ref,
     o0_ref, o1_ref, o2_ref, l0_ref, l1_ref, l2_ref, gpool_ref, gattn_ref) = refs[:15]
    x_refs = refs[15:15 + n_x]
    (gate_ref, wgrp_ref, pscale_ref, wpp_ref, wpa_ref, wout_ref,
     ngain_ref, nshift_ref, nscale_ref) = refs[15 + n_x:24 + n_x]
    outs_and_scratch = refs[24 + n_x:]
    if final:
        y_ref, band_ref, ext_ref, a0_ref, b0_ref, a1_ref, b1_ref, m_ref, oi_ref, li_ref, wt_ref, sq_ref = outs_and_scratch
    else:
        out_ref, h_ref, band_ref, ext_ref, a0_ref, b0_ref, a1_ref, b1_ref, m_ref, oi_ref, li_ref, wt_ref, sq_ref = outs_and_scratch
    tm = gpool_ref.shape[0]
    hl = POOL_HALO
    gw = GROUP_WIDTH
    hpg = HEADS_PER_GROUP
    s = pl.program_id(0)

    @pl.when(s == 0)
    def _():
        i = lax.broadcasted_iota(jnp.int32, (tm, tm + 2 * hl), 0)
        e = lax.broadcasted_iota(jnp.int32, (tm, tm + 2 * hl), 1)
        off = e - hl - i
        for g, w in enumerate(POOL_WINDOWS):
            band_ref[g] = jnp.where((off >= -(w // 2)) & (off < w // 2), 1.0, 0.0).astype(BF16)
        a0_ref[...] = jnp.zeros_like(a0_ref)
        b0_ref[...] = jnp.zeros_like(b0_ref)

    def step(a_cur, b_cur, a_nxt, b_nxt):
        back_tile = jnp.maximum(s - 1, 0) + block_off
        front_tile = jnp.minimum(s, n_tiles - 1) + block_off
        _, t0, length = _locate(front_tile * tm, lay)
        n_pieces = D_MODEL // MERGE_PIECE

        def piece(c):
            return slice(c * MERGE_PIECE, (c + 1) * MERGE_PIECE)

        def head(hh):
            return slice(hh * HEAD_DIM, (hh + 1) * HEAD_DIM)

        def front_prepare():
            zeros = jnp.zeros((hl, POOL_WIDTH), BF16)
            ext_ref[0:hl] = jnp.where(t0 == 0, zeros, pprev_ref[...])
            ext_ref[hl:hl + tm] = pin_ref[...]
            ext_ref[hl + tm:] = jnp.where(t0 + tm == length, zeros, pnext_ref[...])
            for slot, (o_ref, l_ref, dil) in enumerate(((o1_ref, l1_ref, DILATIONS[1]), (o2_ref, l2_ref, DILATIONS[2]))):
                for r in range(dil):
                    rows = pl.ds(r, tm // dil, stride=dil)
                    for hh in range(hpg):
                        oi_ref[slot * hpg + hh, rows, :] = o_ref[r, :, head(hh)].astype(F32)
                    li_ref[slot, rows, :] = l_ref[r]
            lses = (l0_ref[...], li_ref[0], li_ref[1])
            mx = jnp.maximum(jnp.maximum(lses[0], lses[1]), lses[2])
            es = [jnp.exp(l - mx) for l in lses]
            inv_den = 1.0 / (es[0] + es[1] + es[2])
            for g in range(len(DILATIONS)):
                wt_ref[g] = es[g] * inv_den

        def front_head(g, hh):
            def run():
                ag_ref = (ag0_ref, ag1_ref, ag2_ref)[g]
                o = o0_ref[:, head(hh)].astype(F32) if g == 0 else oi_ref[(g - 1) * hpg + hh]
                gate_half = ag_ref[:, head(hh)].astype(F32)
                b = o * wt_ref[g, :, hh:hh + 1] * (gate_half * _one_plus_tanh(gate_half))
                b_nxt[:, g * gw + hh * HEAD_DIM:g * gw + (hh + 1) * HEAD_DIM] = b.astype(BF16)
            return run

        def front_pool(g):
            def run():
                w = POOL_WINDOWS[g]
                cs = slice(g * POOL_GROUP, (g + 1) * POOL_GROUP)
                t = t0 + lax.broadcasted_iota(jnp.int32, (tm, 1), 0)
                acc = jnp.dot(band_ref[g], ext_ref[:, cs], preferred_element_type=F32)
                cnt = (jnp.minimum(t + w // 2, length) - jnp.maximum(t - w // 2, 0)).astype(F32)
                p = acc * (1.0 / cnt) - pin_ref[:, cs].astype(F32)
                y = jnp.dot(p.astype(BF16), wgrp_ref[g], preferred_element_type=F32)
                gate_half = pgate_ref[:, cs].astype(F32)
                a = y * pscale_ref[:, cs] * (gate_half * _one_plus_tanh(gate_half))
                a_nxt[:, cs] = a.astype(BF16)
            return run

        front = [front_head(g, hh) for g in range(len(DILATIONS)) for hh in range(hpg)]
        front += [front_pool(g) for g in range(len(POOL_WINDOWS))]

        x_dst = y_ref if final else out_ref

        def back_merge(c):
            def run():
                pp = jnp.dot(a_cur[...], wpp_ref[:, piece(c)], preferred_element_type=F32)
                pa = jnp.dot(b_cur[...], wpa_ref[:, piece(c)], preferred_element_type=F32)
                merged = (_one_plus_tanh(gpool_ref[:, piece(c)].astype(F32)) * pp
                          + _one_plus_tanh(gattn_ref[:, piece(c)].astype(F32)) * pa)
                m_ref[:, piece(c)] = merged.astype(BF16)
            return run

        def back_out(c):
            def run():
                out = jnp.dot(m_ref[...], wout_ref[:, piece(c)], preferred_element_type=F32)
                if split_x:
                    x = jnp.where(back_tile * tm < lay[0] * lay[1], x_refs[0][:, piece(c)], x_refs[1][:, piece(c)])
                else:
                    x = x_refs[0][:, piece(c)]
                x_new = x + gate_ref[0, :, piece(c)] * out
                x_dst[:, piece(c)] = x_new
                sq_ref[c] = jnp.sum(x_new * x_new, axis=-1, keepdims=True)
            return run

        back = [back_merge(c) for c in range(n_pieces)] + [back_out(c) for c in range(n_pieces)]

        front_prepare()
        for k in range(max(len(back), len(front))):
            if k < len(back):
                back[k]()
            if k < len(front):
                front[k]()

        sq = sq_ref[0]
        for c in range(1, n_pieces):
            sq = sq + sq_ref[c]
        inv_rms = lax.rsqrt(sq * (1.0 / D_MODEL) + NORM_EPS)
        for c in range(n_pieces):
            xn = x_dst[:, piece(c)] * inv_rms * ngain_ref[:, piece(c)]
            if final:
                y_ref[:, piece(c)] = xn
            else:
                h_ref[:, piece(c)] = (xn * (1.0 + nscale_ref[0, :, piece(c)]) + nshift_ref[0, :, piece(c)]).astype(BF16)

    @pl.when(s % 2 == 0)
    def _():
        step(a0_ref, b0_ref, a1_ref, b1_ref)

    @pl.when(s % 2 == 1)
    def _():
        step(a1_ref, b1_ref, a0_ref, b0_ref)


def _merge(z, attn, x, mod_rows, layer, w_grp, pool_scale, w_pp, w_pa, w_out, next_gain, gain_index, lay, *,
           final=False, row_start=0, rows=None):
    split_x = isinstance(x, (tuple, list))
    d = D_MODEL
    t = z.shape[0]
    rows = t if rows is None else rows
    tm = MERGE_TM
    hb = POOL_HALO
    assert row_start % tm == 0 and rows % tm == 0
    off = row_start // tm
    n_tiles = rows // tm
    halo_last = t // hb - 1
    (o0, l0), (o1, l1), (o2, l2) = attn
    d1, d2 = DILATIONS[1], DILATIONS[2]

    def front(s):
        return jnp.minimum(s, n_tiles - 1) + off

    def back(s):
        return jnp.maximum(s - 1, 0) + off

    def zcols(tile, width, col):
        assert col % width == 0
        return pl.BlockSpec((tm, width), lambda s: (tile(s), col // width))

    def const(stacked, index):
        tail = stacked.shape[1:]
        return pl.BlockSpec((None, *tail), lambda s: (index,) + (0,) * len(tail), pipeline_mode=pl.Buffered(1))

    def row(tile, width):
        return pl.BlockSpec((tm, width), lambda s: (tile(s), 0))

    def cls(dil, width):
        return pl.BlockSpec((dil, tm // dil, width), lambda s: (0, front(s), 0))

    next_layer = layer if final else layer + 1
    out_row = pl.BlockSpec((tm, d), lambda s: (jnp.maximum(s - 1, 0), 0))
    if final:
        out_specs = out_row
        out_shape = jax.ShapeDtypeStruct((rows, d), F32)
    else:
        out_specs = [out_row, out_row]
        out_shape = [jax.ShapeDtypeStruct((rows, d), F32), jax.ShapeDtypeStruct((rows, d), BF16)]
    x_specs = _stream_specs(tm, lay, back) if split_x else [row(back, d)]
    x_args = tuple(x) if split_x else (x,)

    return pl.pallas_call(
        functools.partial(_merge_kernel, lay=lay, block_off=off, n_tiles=n_tiles, final=final, split_x=split_x),
        grid=(n_tiles + 1,),
        in_specs=[
            zcols(front, POOL_WIDTH, ZC_POOL_IN),
            pl.BlockSpec((hb, POOL_WIDTH),
                         lambda s: (jnp.maximum(front(s) * (tm // hb) - 1, 0), ZC_POOL_IN // POOL_WIDTH)),
            pl.BlockSpec((hb, POOL_WIDTH),
                         lambda s: (jnp.minimum((front(s) + 1) * (tm // hb), halo_last), ZC_POOL_IN // POOL_WIDTH)),
            zcols(front, POOL_WIDTH, ZC_POOL_GATE),
            zcols(front, GROUP_WIDTH, ZC_AG0),
            zcols(front, GROUP_WIDTH, ZC_AG1),
            zcols(front, GROUP_WIDTH, ZC_AG2),
            row(front, GROUP_WIDTH), cls(d1, GROUP_WIDTH), cls(d2, GROUP_WIDTH),
            row(front, HEAD_DIM), cls(d1, HEAD_DIM), cls(d2, HEAD_DIM),
            zcols(back, D_MODEL, ZC_G_POOL),
            zcols(back, D_MODEL, ZC_G_ATTN),
            *x_specs,
            _mod_spec(layer, 2, tm, lay, back),
            const(w_grp, layer), const(pool_scale, layer), const(w_pp, layer), const(w_pa, layer),
            const(w_out, layer), const(next_gain, gain_index),
            _mod_spec(next_layer, 0, tm, lay, back),
            _mod_spec(next_layer, 1, tm, lay, back),
        ],
        out_specs=out_specs,
        out_shape=out_shape,
        scratch_shapes=[
            pltpu.VMEM((len(POOL_WINDOWS), tm, tm + 2 * POOL_HALO), BF16),
            pltpu.VMEM((tm + 2 * POOL_HALO, POOL_WIDTH), BF16),
            pltpu.VMEM((tm, POOL_WIDTH), BF16),
            pltpu.VMEM((tm, ATTN_WIDTH), BF16),
            pltpu.VMEM((tm, POOL_WIDTH), BF16),
            pltpu.VMEM((tm, ATTN_WIDTH), BF16),
            pltpu.VMEM((tm, D_MODEL), BF16),
            pltpu.VMEM((2 * HEADS_PER_GROUP, tm, HEAD_DIM), F32),
            pltpu.VMEM((2, tm, HEAD_DIM), F32),
            pltpu.VMEM((len(DILATIONS), tm, HEAD_DIM), F32),
            pltpu.VMEM((D_MODEL // MERGE_PIECE, tm, 1), F32),
        ],
        compiler_params=pltpu.CompilerParams(
            dimension_semantics=("arbitrary",), vmem_limit_bytes=VMEM_LIMIT_BYTES),
        name="merge_final" if final else "merge",
    )(z, z, z, z, z, z, z, o0, o1, o2, l0, l1, l2, z, z, *x_args, mod_rows,
      w_grp, pool_scale, w_pp, w_pa, w_out, next_gain, mod_rows, mod_rows)


def _prepare_in_proj_weight(w):
    n_chunks = IN_WIDTH // GROUP_WIDTH
    scale = jnp.repeat(jnp.asarray([0.5 if c in HALVED_CHUNKS else 1.0 for c in range(n_chunks)], F32), GROUP_WIDTH)
    return (w * scale).astype(BF16)


def kernel(x_prompt, x_sample, c_prompt, c_sample, norm_gain, w_ada, b_ada, w_in, w_pool_grp, pool_scale,
           w_proj_pool, w_proj_attn, w_out, final_gain):
    bp, sp, d = x_prompt.shape
    bs, ss, _ = x_sample.shape
    assert d == D_MODEL
    lay = (bp, sp, bs, ss)
    depth = w_in.shape[0]
    nseq = bp + bs
    gw = GROUP_WIDTH
    for s in (sp, ss):
        assert s % INPROJ_TM == 0 and s % MERGE_TM == 0 and s % (max(DILATIONS) * ATTN_SUB) == 0

    x = (x_prompt.reshape(bp * sp, d), x_sample.reshape(bs * ss, d))
    c = jnp.concatenate([c_prompt, c_sample], axis=0)
    mod_rows = _modulation(c, w_ada, b_ada).reshape(depth * nseq * 3, 1, d)
    cos, sin = _rope_tables(max(sp, ss))

    w_in_b = _prepare_in_proj_weight(w_in)
    weights = (w_pool_grp.astype(BF16), pool_scale.reshape(depth, 1, POOL_WIDTH), (0.5 * w_proj_pool).astype(BF16),
               (0.5 * w_proj_attn).astype(BF16), w_out.astype(BF16))
    gains = norm_gain.reshape(depth, 1, d)

    h = _prenorm(x, norm_gain[0], mod_rows, lay)
    for layer in range(depth):
        z, qk1, qk2, v1, v2 = _in_proj(h, w_in_b, layer, cos, sin, lay)
        o0, l0 = _band_attention(z[None], z[None], ZC_Q0 // gw, ZC_K0 // gw, ZC_V0 // gw, DILATIONS[0], lay)
        attn = [
            (o0[0], l0[0]),
            _band_attention(qk1, v1, 0, 1, 0, DILATIONS[1], lay),
            _band_attention(qk2, v2, 0, 1, 0, DILATIONS[2], lay),
        ]
        if layer + 1 < depth:
            x, h = _merge(z, attn, x, mod_rows, layer, *weights, gains, layer + 1, lay)
        else:
            last = (z, attn, x, mod_rows, layer, *weights, final_gain.reshape(1, 1, d), 0, lay)
            y_prompt = _merge(*last, final=True, row_start=0, rows=bp * sp)
            y_sample = _merge(*last, final=True, row_start=bp * sp, rows=bs * ss)
    return (y_prompt.reshape(bp, sp, d), y_sample.reshape(bs, ss, d))
```

```python
import functools
import math

import jax
import jax.numpy as jnp
from jax import lax
from jax.experimental import pallas as pl
from jax.experimental.pallas import tpu as pltpu

F32 = jnp.float32
BF16 = jnp.bfloat16

D_MODEL = 2048
POOL_WIDTH = D_MODEL // 2
POOL_WINDOWS = (2, 4, 8, 16)
POOL_GROUP = POOL_WIDTH // len(POOL_WINDOWS)
POOL_HALO = 64
assert POOL_HALO >= max(POOL_WINDOWS) // 2
HEAD_DIM = 128
ATTN_GROUPS = ((128, 1), (512, 4), (2048, 16))
DILATIONS = tuple(dil for _, dil in ATTN_GROUPS)
HEADS_PER_GROUP = 4
GROUP_WIDTH = HEADS_PER_GROUP * HEAD_DIM
ATTN_WIDTH = GROUP_WIDTH * len(ATTN_GROUPS)
BAND_RADIUS = 64
assert all(win // (2 * dil) == BAND_RADIUS for win, dil in ATTN_GROUPS)
assert DILATIONS[0] == 1 and len(DILATIONS) == 3
ROPE_THETA = 10000.0
NORM_EPS = 1e-6
NEG_BIG = -1e30
IN_WIDTH = 2 * POOL_WIDTH + 4 * ATTN_WIDTH + 2 * D_MODEL

_CH_POOL_IN, _CH_POOL_GATE, _CH_Q, _CH_K, _CH_V, _CH_AG, _CH_GP, _CH_GA = 0, 2, 4, 7, 10, 13, 16, 20
IN_PROJ_CHUNK_ORDER = (
    _CH_Q, _CH_K, _CH_Q + 1, _CH_K + 1, _CH_Q + 2, _CH_K + 2, _CH_V + 1, _CH_V + 2,
    _CH_V, _CH_AG, *range(_CH_GP, _CH_GP + 4), *range(_CH_GA, _CH_GA + 4),
    _CH_POOL_IN, _CH_POOL_IN + 1, _CH_POOL_GATE, _CH_POOL_GATE + 1, _CH_AG + 1, _CH_AG + 2)
assert sorted(IN_PROJ_CHUNK_ORDER) == list(range(IN_WIDTH // GROUP_WIDTH))
HALVED_CHUNKS = frozenset((*range(_CH_POOL_GATE, _CH_POOL_GATE + 2), *range(_CH_AG, _CH_AG + 3),
                           *range(_CH_GP, _CH_GP + 4), *range(_CH_GA, _CH_GA + 4)))
IN_PROJ_CHUNK_SCALE = tuple(0.5 if c in HALVED_CHUNKS else 1.0 for c in IN_PROJ_CHUNK_ORDER)
N_SPECIAL_TILES = 4
Z_WIDTH = IN_WIDTH - 6 * GROUP_WIDTH
ZC_Q0, ZC_K0, ZC_V0, ZC_AG0 = 0, GROUP_WIDTH, 2 * GROUP_WIDTH, 3 * GROUP_WIDTH
ZC_G_POOL = 4 * GROUP_WIDTH
ZC_G_ATTN = ZC_G_POOL + D_MODEL
ZC_POOL_IN = ZC_G_ATTN + D_MODEL
ZC_POOL_GATE = ZC_POOL_IN + POOL_WIDTH
ZC_AG1 = ZC_POOL_GATE + POOL_WIDTH
ZC_AG2 = ZC_AG1 + GROUP_WIDTH

VMEM_LIMIT_BYTES = 60 * 1024 * 1024

INPROJ_TM = 1024
INPROJ_TN = 2 * GROUP_WIDTH
INPROJ_PIECE = 256
ATTN_BQ = 512
ATTN_SUB = 128
MERGE_TM = 256
MERGE_PIECE = 256
NORM_TM = 512


def _one_plus_tanh(x_half):
    return 1.0 + jnp.tanh(x_half)


def _mod_norm(x, gain, scale, shift):
    xn = x * lax.rsqrt(jnp.mean(x * x, axis=-1, keepdims=True) + NORM_EPS) * gain
    return xn * (1.0 + scale) + shift


def _locate(row0, lay, dil=1):
    bp, sp, bs, ss = lay
    rows_p, lp, ls = bp * sp // dil, sp // dil, ss // dil
    in_p = row0 < rows_p
    seq = jnp.where(in_p, row0 // lp, bp + (row0 - rows_p) // ls)
    local = jnp.where(in_p, row0 % lp, (row0 - rows_p) % ls)
    length = jnp.where(in_p, lp, ls)
    return seq, local, length


def _select_by_tile(j, table):
    out = table[0]
    for k in range(1, len(table)):
        out = jnp.where(j == k, table[k], out)
    return out


def _mod_spec(layer, which, tm, lay, tile=lambda i: i):
    nseq = lay[0] + lay[2]

    def idx(i):
        seq, _, _ = _locate(tile(i) * tm, lay)
        return ((layer * nseq + seq) * 3 + which, 0, 0)

    return pl.BlockSpec((1, 1, D_MODEL), idx)


def _stream_specs(tm, lay, tile=lambda i: i):
    tiles_p = lay[0] * lay[1] // tm
    tiles_s = lay[2] * lay[3] // tm
    return [
        pl.BlockSpec((tm, D_MODEL), lambda i: (jnp.minimum(tile(i), tiles_p - 1), 0)),
        pl.BlockSpec((tm, D_MODEL), lambda i: (jnp.clip(tile(i) - tiles_p, 0, tiles_s - 1), 0)),
    ]


def _stream_tile(xp_ref, xs_ref, tile, tm, lay):
    return jnp.where(tile * tm < lay[0] * lay[1], xp_ref[...], xs_ref[...])


def _mod_kernel(c_ref, w_ref, b_ref, o_ref):
    c = c_ref[...].astype(BF16)
    w = w_ref[0].astype(BF16)
    o_ref[0] = jnp.dot(c, w, preferred_element_type=F32) + b_ref[0]


def _modulation(c, w_ada, b_ada):
    depth, d, n = w_ada.shape
    nseq = c.shape[0]
    tn = 1536
    return pl.pallas_call(
        _mod_kernel,
        grid=(depth, n // tn),
        in_specs=[
            pl.BlockSpec((nseq, d), lambda l, j: (0, 0)),
            pl.BlockSpec((1, d, tn), lambda l, j: (l, 0, j)),
            pl.BlockSpec((1, 1, tn), lambda l, j: (l, 0, j)),
        ],
        out_specs=pl.BlockSpec((1, nseq, tn), lambda l, j: (l, 0, j)),
        out_shape=jax.ShapeDtypeStruct((depth, nseq, n), F32),
        compiler_params=pltpu.CompilerParams(
            dimension_semantics=("arbitrary", "arbitrary"), vmem_limit_bytes=VMEM_LIMIT_BYTES),
        name="modulation",
    )(c, w_ada, b_ada.reshape(depth, 1, n))


def _rope_table_kernel(cos_ref, sin_ref):
    s = cos_ref.shape[0]
    half = HEAD_DIM // 2
    pos = lax.broadcasted_iota(jnp.int32, (s, HEAD_DIM), 0).astype(F32)
    lane = lax.broadcasted_iota(jnp.int32, (s, HEAD_DIM), 1)
    idx = jnp.where(lane < half, lane, lane - half).astype(F32)
    inv = jnp.exp(idx * (-math.log(ROPE_THETA) / half))
    ang = pos * inv
    cos_ref[...] = jnp.cos(ang)
    sin_ref[...] = jnp.where(lane < half, -jnp.sin(ang), jnp.sin(ang))


def _rope_tables(s):
    return pl.pallas_call(
        _rope_table_kernel,
        out_shape=(jax.ShapeDtypeStruct((s, HEAD_DIM), F32), jax.ShapeDtypeStruct((s, HEAD_DIM), F32)),
        name="rope_tables",
    )()


def _prenorm_kernel(xp_ref, xs_ref, g_ref, shift_ref, scale_ref, h_ref, *, lay):
    tm = h_ref.shape[0]
    x = _stream_tile(xp_ref, xs_ref, pl.program_id(0), tm, lay)
    h_ref[...] = _mod_norm(x, g_ref[...], scale_ref[0], shift_ref[0]).astype(BF16)


def _prenorm(x_streams, gain, mod_rows, lay):
    d = D_MODEL
    t = lay[0] * lay[1] + lay[2] * lay[3]
    tm = NORM_TM
    return pl.pallas_call(
        functools.partial(_prenorm_kernel, lay=lay),
        grid=(t // tm,),
        in_specs=[
            *_stream_specs(tm, lay),
            pl.BlockSpec((1, d), lambda i: (0, 0)),
            _mod_spec(0, 0, tm, lay),
            _mod_spec(0, 1, tm, lay),
        ],
        out_specs=pl.BlockSpec((tm, d), lambda i: (i, 0)),
        out_shape=jax.ShapeDtypeStruct((t, d), BF16),
        compiler_params=pltpu.CompilerParams(
            dimension_semantics=("parallel",), vmem_limit_bytes=VMEM_LIMIT_BYTES),
        name="prenorm",
    )(*x_streams, gain.reshape(1, d), mod_rows, mod_rows)


def _inproj_kernel(h_ref, wa_ref, wb_ref, cos_ref, sin_ref, z_ref, qk1_ref, qk2_ref, v1_ref, v2_ref,
                   w_ref, s_ref, t_ref):
    j = pl.program_id(0)
    n_chunks, tm, _ = s_ref.shape
    d1, d2 = DILATIONS[1], DILATIONS[2]
    hpg = HEADS_PER_GROUP
    per_piece = INPROJ_PIECE // HEAD_DIM

    @pl.when(pl.program_id(1) == 0)
    def _():
        for half, src_ref in enumerate((wa_ref, wb_ref)):
            w_ref[half] = (src_ref[...] * _select_by_tile(j, IN_PROJ_CHUNK_SCALE[half::2])).astype(BF16)

    def chunk(hh):
        return slice(hh * HEAD_DIM, (hh + 1) * HEAD_DIM)

    def pieces():
        pieces_per_half = GROUP_WIDTH // INPROJ_PIECE
        for c in range(n_chunks // per_piece):
            c_in = c % pieces_per_half
            cols = slice(c_in * INPROJ_PIECE, (c_in + 1) * INPROJ_PIECE)
            acc = jnp.dot(h_ref[...], w_ref[c // pieces_per_half, :, cols], preferred_element_type=F32)
            for k in range(per_piece):
                yield c * per_piece + k, acc[:, chunk(k)]

    def rope(xh):
        return xh * cos_ref[...] + pltpu.roll(xh, HEAD_DIM // 2, 1) * sin_ref[...]

    def to_classes(dst_ref, dil, hh, c_out):
        if dil == d1 * d1:
            for r1 in range(d1):
                t_ref[hh % 2, r1] = s_ref[hh, pl.ds(r1, tm // d1, stride=d1), :]
            for r in range(dil):
                rows = pl.ds(r // d1, tm // dil, stride=d1)
                dst_ref[r, :, chunk(c_out)] = t_ref[hh % 2, r % d1, rows, :].astype(BF16)
        else:
            for r in range(dil):
                dst_ref[r, :, chunk(c_out)] = s_ref[hh, pl.ds(r, tm // dil, stride=dil), :].astype(BF16)

    @pl.when(j == 0)
    def _():
        for hh, xh in pieces():
            z_ref[:, chunk(hh)] = rope(xh).astype(BF16)

    @pl.when(j == 1)
    def _():
        for hh, xh in pieces():
            s_ref[hh] = rope(xh)
            to_classes(qk1_ref, d1, hh, hh)

    @pl.when(j == 2)
    def _():
        for hh, xh in pieces():
            s_ref[hh] = rope(xh)
            to_classes(qk2_ref, d2, hh, hh)

    @pl.when(j == 3)
    def _():
        for hh, xh in pieces():
            s_ref[hh] = xh
            if hh < hpg:
                to_classes(v1_ref, d1, hh, hh)
            else:
                to_classes(v2_ref, d2, hh, hh - hpg)

    @pl.when(j >= N_SPECIAL_TILES)
    def _():
        for hh, xh in pieces():
            z_ref[:, chunk(hh)] = xh.astype(BF16)


def _in_proj(h, w_in, layer, cos, sin, lay):
    t, d = h.shape
    n = w_in.shape[2]
    tm, tn, gw = INPROJ_TM, INPROJ_TN, GROUP_WIDTH
    d1, d2 = DILATIONS[1], DILATIONS[2]
    last = t // tm - 1

    def rope_idx(j, i):
        _, local, _ = _locate(i * tm, lay)
        return (local // tm, 0)

    def z_idx(j, i):
        special = (j >= 1) & (j < N_SPECIAL_TILES)
        return (jnp.where(special, last, i), jnp.maximum(j - (N_SPECIAL_TILES - 1), 0))

    def cls_spec(dil, width, written_at):
        return pl.BlockSpec(
            (dil, tm // dil, width),
            lambda j, i: (0, jnp.where(j < written_at, 0, jnp.where(j == written_at, i, last)), 0))

    def cls_shape(dil, width):
        return jax.ShapeDtypeStruct((dil, t // dil, width), BF16)

    def w_half_spec(half):
        return pl.BlockSpec((None, d, gw),
                            lambda j, i: (layer, 0, _select_by_tile(j, IN_PROJ_CHUNK_ORDER[half::2])))

    return pl.pallas_call(
        _inproj_kernel,
        grid=(n // tn, t // tm),
        in_specs=[
            pl.BlockSpec((tm, d), lambda j, i: (i, 0)),
            w_half_spec(0),
            w_half_spec(1),
            pl.BlockSpec((tm, HEAD_DIM), rope_idx),
            pl.BlockSpec((tm, HEAD_DIM), rope_idx),
        ],
        out_specs=[
            pl.BlockSpec((tm, tn), z_idx),
            cls_spec(d1, 2 * gw, 1), cls_spec(d2, 2 * gw, 2), cls_spec(d1, gw, 3), cls_spec(d2, gw, 3),
        ],
        out_shape=[
            jax.ShapeDtypeStruct((t, Z_WIDTH), BF16),
            cls_shape(d1, 2 * gw), cls_shape(d2, 2 * gw), cls_shape(d1, gw), cls_shape(d2, gw),
        ],
        scratch_shapes=[pltpu.VMEM((2, d, gw), BF16),
                        pltpu.VMEM((tn // HEAD_DIM, tm, HEAD_DIM), F32),
                        pltpu.VMEM((2, d1, tm // d1, HEAD_DIM), F32)],
        compiler_params=pltpu.CompilerParams(
            dimension_semantics=("arbitrary", "arbitrary"), vmem_limit_bytes=VMEM_LIMIT_BYTES),
        name="in_proj",
    )(h, w_in, w_in, cos, sin)


def _attn_kernel(q_ref, kp_ref, km_ref, kn_ref, vp_ref, vm_ref, vn_ref, o_ref, lse_ref, k_buf, v_buf, *, dil, lay):
    bq = q_ref.shape[0]
    r = BAND_RADIUS
    k_buf[0:r] = kp_ref[...]
    k_buf[r:r + bq] = km_ref[...]
    k_buf[r + bq:] = kn_ref[...]
    v_buf[0:r] = vp_ref[...]
    v_buf[r:r + bq] = vm_ref[...]
    v_buf[r + bq:] = vn_ref[...]

    nk = ATTN_SUB + 2 * r
    qi = lax.broadcasted_iota(jnp.int32, (ATTN_SUB, nk), 0)
    kj = lax.broadcasted_iota(jnp.int32, (ATTN_SUB, nk), 1) - r
    in_band = jnp.abs(kj - qi) <= r
    lane = lax.broadcasted_iota(jnp.int32, (ATTN_SUB, HEAD_DIM), 1)
    inv_sqrt = 1.0 / math.sqrt(HEAD_DIM)

    for sb in range(bq // ATTN_SUB):
        a = sb * ATTN_SUB
        _, local, length = _locate(pl.program_id(1) * bq + a, lay, dil)
        mask = in_band & (kj + local >= 0) & (kj + local < length)
        lse_tile = jnp.zeros((ATTN_SUB, HEAD_DIM), F32)
        for hh in range(HEADS_PER_GROUP):
            cs = slice(hh * HEAD_DIM, (hh + 1) * HEAD_DIM)
            q = q_ref[a:a + ATTN_SUB, cs]
            k = k_buf[a:a + nk, cs]
            v = v_buf[a:a + nk, cs]
            s = lax.dot_general(q, k, (((1,), (1,)), ((), ())), preferred_element_type=F32) * inv_sqrt
            s = jnp.where(mask, s, NEG_BIG)
            m = jnp.max(s, axis=-1, keepdims=True)
            p = jnp.exp(s - m)
            l = jnp.sum(p, axis=-1, keepdims=True)
            o = jnp.dot(p.astype(BF16), v, preferred_element_type=F32) * (1.0 / l)
            o_ref[a:a + ATTN_SUB, cs] = o.astype(BF16)
            lse_tile = jnp.where(lane == hh, m + jnp.log(l), lse_tile)
        lse_ref[a:a + ATTN_SUB, :] = lse_tile


def _band_attention(qk, v, q_col, k_col, v_col, dil, lay):
    rows = qk.shape[1]
    bq, r, gw = ATTN_BQ, BAND_RADIUS, GROUP_WIDTH
    assert rows % bq == 0 and bq % r == 0
    halo_last = rows // r - 1

    def main(col):
        return pl.BlockSpec((None, bq, gw), lambda c, n: (c, n, col))

    def prev(col):
        return pl.BlockSpec((None, r, gw), lambda c, n: (c, jnp.maximum(n * (bq // r) - 1, 0), col))

    def nxt(col):
        return pl.BlockSpec((None, r, gw), lambda c, n: (c, jnp.minimum((n + 1) * (bq // r), halo_last), col))

    return pl.pallas_call(
        functools.partial(_attn_kernel, dil=dil, lay=lay),
        grid=(dil, rows // bq),
        in_specs=[main(q_col), prev(k_col), main(k_col), nxt(k_col), prev(v_col), main(v_col), nxt(v_col)],
        out_specs=[
            pl.BlockSpec((None, bq, gw), lambda c, n: (c, n, 0)),
            pl.BlockSpec((None, bq, HEAD_DIM), lambda c, n: (c, n, 0)),
        ],
        out_shape=[
            jax.ShapeDtypeStruct((dil, rows, gw), BF16),
            jax.ShapeDtypeStruct((dil, rows, HEAD_DIM), F32),
        ],
        scratch_shapes=[pltpu.VMEM((bq + 2 * r, gw), BF16), pltpu.VMEM((bq + 2 * r, gw), BF16)],
        compiler_params=pltpu.CompilerParams(
            dimension_semantics=("parallel", "parallel"), vmem_limit_bytes=VMEM_LIMIT_BYTES),
        name=f"band_attn_d{dil}",
    )(qk, qk, qk, qk, v, v, v)


def _merge_kernel(*refs, lay, block_off, n_tiles, final, split_x):
    n_x = 2 if split_x else 1
    (pin_ref, pprev_ref, pnext_ref, pgate_ref, ag0_ref, ag1_ref, ag2_ref,
     o0_ref, o1_ref, o2_ref, l0_ref, l1_ref, l2_ref, gpool_ref, gattn_ref) = refs[:15]
    x_refs = refs[15:15 + n_x]
    (gate_ref, wgrp_ref, pscale_ref, wpp_ref, wpa_ref, wout_ref,
     ngain_ref, nshift_ref, nscale_ref) = refs[15 + n_x:24 + n_x]
    outs_and_scratch = refs[24 + n_x:]
    if final:
        y_ref, band_ref, ext_ref, a0_ref, b0_ref, a1_ref, b1_ref, m_ref, oi_ref, li_ref, wt_ref, sq_ref = outs_and_scratch
    else:
        out_ref, h_ref, band_ref, ext_ref, a0_ref, b0_ref, a1_ref, b1_ref, m_ref, oi_ref, li_ref, wt_ref, sq_ref = outs_and_scratch
    tm = gpool_ref.shape[0]
    hl = POOL_HALO
    gw = GROUP_WIDTH
    hpg = HEADS_PER_GROUP
    s = pl.program_id(0)

    @pl.when(s == 0)
    def _():
        i = lax.broadcasted_iota(jnp.int32, (tm, tm + 2 * hl), 0)
        e = lax.broadcasted_iota(jnp.int32, (tm, tm + 2 * hl), 1)
        off = e - hl - i
        for g, w in enumerate(POOL_WINDOWS):
            band_ref[g] = jnp.where((off >= -(w // 2)) & (off < w // 2), 1.0, 0.0).astype(BF16)
        a0_ref[...] = jnp.zeros_like(a0_ref)
        b0_ref[...] = jnp.zeros_like(b0_ref)

    def step(a_cur, b_cur, a_nxt, b_nxt):
        back_tile = jnp.maximum(s - 1, 0) + block_off
        front_tile = jnp.minimum(s, n_tiles - 1) + block_off
        _, t0, length = _locate(front_tile * tm, lay)
        n_pieces = D_MODEL // MERGE_PIECE

        def piece(c):
            return slice(c * MERGE_PIECE, (c + 1) * MERGE_PIECE)

        def head(hh):
            return slice(hh * HEAD_DIM, (hh + 1) * HEAD_DIM)

        def front_prepare():
            zeros = jnp.zeros((hl, POOL_WIDTH), BF16)
            ext_ref[0:hl] = jnp.where(t0 == 0, zeros, pprev_ref[...])
            ext_ref[hl:hl + tm] = pin_ref[...]
            ext_ref[hl + tm:] = jnp.where(t0 + tm == length, zeros, pnext_ref[...])
            for slot, (o_ref, l_ref, dil) in enumerate(((o1_ref, l1_ref, DILATIONS[1]), (o2_ref, l2_ref, DILATIONS[2]))):
                for r in range(dil):
                    rows = pl.ds(r, tm // dil, stride=dil)
                    for hh in range(hpg):
                        oi_ref[slot * hpg + hh, rows, :] = o_ref[r, :, head(hh)].astype(F32)
                    li_ref[slot, rows, :] = l_ref[r]
            lses = (l0_ref[...], li_ref[0], li_ref[1])
            mx = jnp.maximum(jnp.maximum(lses[0], lses[1]), lses[2])
            es = [jnp.exp(l - mx) for l in lses]
            inv_den = 1.0 / (es[0] + es[1] + es[2])
            for g in range(len(DILATIONS)):
                wt_ref[g] = es[g] * inv_den

        def front_head(g, hh):
            def run():
                ag_ref = (ag0_ref, ag1_ref, ag2_ref)[g]
                o = o0_ref[:, head(hh)].astype(F32) if g == 0 else oi_ref[(g - 1) * hpg + hh]
                gate_half = ag_ref[:, head(hh)].astype(F32)
                b = o * wt_ref[g, :, hh:hh + 1] * (gate_half * _one_plus_tanh(gate_half))
                b_nxt[:, g * gw + hh * HEAD_DIM:g * gw + (hh + 1) * HEAD_DIM] = b.astype(BF16)
            return run

        def front_pool(g):
            def run():
                w = POOL_WINDOWS[g]
                cs = slice(g * POOL_GROUP, (g + 1) * POOL_GROUP)
                t = t0 + lax.broadcasted_iota(jnp.int32, (tm, 1), 0)
                acc = jnp.dot(band_ref[g], ext_ref[:, cs], preferred_element_type=F32)
                cnt = (jnp.minimum(t + w // 2, length) - jnp.maximum(t - w // 2, 0)).astype(F32)
                p = acc * (1.0 / cnt) - pin_ref[:, cs].astype(F32)
                y = jnp.dot(p.astype(BF16), wgrp_ref[g], preferred_element_type=F32)
                gate_half = pgate_ref[:, cs].astype(F32)
                a = y * pscale_ref[:, cs] * (gate_half * _one_plus_tanh(gate_half))
                a_nxt[:, cs] = a.astype(BF16)
            return run

        front = [front_head(g, hh) for g in range(len(DILATIONS)) for hh in range(hpg)]
        front += [front_pool(g) for g in range(len(POOL_WINDOWS))]

        x_dst = y_ref if final else out_ref

        def back_merge(c):
            def run():
                pp = jnp.dot(a_cur[...], wpp_ref[:, piece(c)], preferred_element_type=F32)
                pa = jnp.dot(b_cur[...], wpa_ref[:, piece(c)], preferred_element_type=F32)
                merged = (_one_plus_tanh(gpool_ref[:, piece(c)].astype(F32)) * pp
                          + _one_plus_tanh(gattn_ref[:, piece(c)].astype(F32)) * pa)
                m_ref[:, piece(c)] = merged.astype(BF16)
            return run

        def back_out(c):
            def run():
                out = jnp.dot(m_ref[...], wout_ref[:, piece(c)], preferred_element_type=F32)
                if split_x:
                    x = jnp.where(back_tile * tm < lay[0] * lay[1], x_refs[0][:, piece(c)], x_refs[1][:, piece(c)])
                else:
                    x = x_refs[0][:, piece(c)]
                x_new = x + gate_ref[0, :, piece(c)] * out
                x_dst[:, piece(c)] = x_new
                sq_ref[:, c:c + 1] = jnp.sum(x_new * x_new, axis=-1, keepdims=True)
            return run

        back = [back_merge(c) for c in range(n_pieces)] + [back_out(c) for c in range(n_pieces)]

        front_prepare()
        for k in range(max(len(back), len(front))):
            if k < len(back):
                back[k]()
            if k < len(front):
                front[k]()

        sq = jnp.sum(sq_ref[:, 0:n_pieces], axis=-1, keepdims=True)
        inv_rms = lax.rsqrt(sq * (1.0 / D_MODEL) + NORM_EPS)
        for c in range(n_pieces):
            xn = x_dst[:, piece(c)] * inv_rms * ngain_ref[:, piece(c)]
            if final:
                y_ref[:, piece(c)] = xn
            else:
                h_ref[:, piece(c)] = (xn * (1.0 + nscale_ref[0, :, piece(c)]) + nshift_ref[0, :, piece(c)]).astype(BF16)

    @pl.when(s % 2 == 0)
    def _():
        step(a0_ref, b0_ref, a1_ref, b1_ref)

    @pl.when(s % 2 == 1)
    def _():
        step(a1_ref, b1_ref, a0_ref, b0_ref)


def _merge(z, attn, x, mod_rows, layer, w_grp, pool_scale, w_pp, w_pa, w_out, next_gain, gain_index, lay, *,
           final=False, row_start=0, rows=None):
    split_x = isinstance(x, (tuple, list))
    d = D_MODEL
    t = z.shape[0]
    rows = t if rows is None else rows
    tm = MERGE_TM
    hb = POOL_HALO
    assert row_start % tm == 0 and rows % tm == 0
    off = row_start // tm
    n_tiles = rows // tm
    halo_last = t // hb - 1
    (o0, l0), (o1, l1), (o2, l2) = attn
    d1, d2 = DILATIONS[1], DILATIONS[2]

    def front(s):
        return jnp.minimum(s, n_tiles - 1) + off

    def back(s):
        return jnp.maximum(s - 1, 0) + off

    def zcols(tile, width, col):
        assert col % width == 0
        return pl.BlockSpec((tm, width), lambda s: (tile(s), col // width))

    def const(stacked, index):
        tail = stacked.shape[1:]
        return pl.BlockSpec((None, *tail), lambda s: (index,) + (0,) * len(tail), pipeline_mode=pl.Buffered(1))

    def row(tile, width):
        return pl.BlockSpec((tm, width), lambda s: (tile(s), 0))

    def cls(dil, width):
        return pl.BlockSpec((dil, tm // dil, width), lambda s: (0, front(s), 0))

    next_layer = layer if final else layer + 1
    out_row = pl.BlockSpec((tm, d), lambda s: (back(s) - off, 0))
    if final:
        out_specs = out_row
        out_shape = jax.ShapeDtypeStruct((rows, d), F32)
    else:
        out_specs = [out_row, out_row]
        out_shape = [jax.ShapeDtypeStruct((rows, d), F32), jax.ShapeDtypeStruct((rows, d), BF16)]
    x_specs = _stream_specs(tm, lay, back) if split_x else [row(back, d)]
    x_args = tuple(x) if split_x else (x,)

    return pl.pallas_call(
        functools.partial(_merge_kernel, lay=lay, block_off=off, n_tiles=n_tiles, final=final, split_x=split_x),
        grid=(n_tiles + 1,),
        in_specs=[
            zcols(front, POOL_WIDTH, ZC_POOL_IN),
            pl.BlockSpec((hb, POOL_WIDTH),
                         lambda s: (jnp.maximum(front(s) * (tm // hb) - 1, 0), ZC_POOL_IN // POOL_WIDTH)),
            pl.BlockSpec((hb, POOL_WIDTH),
                         lambda s: (jnp.minimum((front(s) + 1) * (tm // hb), halo_last), ZC_POOL_IN // POOL_WIDTH)),
            zcols(front, POOL_WIDTH, ZC_POOL_GATE),
            zcols(front, GROUP_WIDTH, ZC_AG0),
            zcols(front, GROUP_WIDTH, ZC_AG1),
            zcols(front, GROUP_WIDTH, ZC_AG2),
            row(front, GROUP_WIDTH), cls(d1, GROUP_WIDTH), cls(d2, GROUP_WIDTH),
            row(front, HEAD_DIM), cls(d1, HEAD_DIM), cls(d2, HEAD_DIM),
            zcols(back, D_MODEL, ZC_G_POOL),
            zcols(back, D_MODEL, ZC_G_ATTN),
            *x_specs,
            _mod_spec(layer, 2, tm, lay, back),
            const(w_grp, layer), const(pool_scale, layer), const(w_pp, layer), const(w_pa, layer),
            const(w_out, layer), const(next_gain, gain_index),
            _mod_spec(next_layer, 0, tm, lay, back),
            _mod_spec(next_layer, 1, tm, lay, back),
        ],
        out_specs=out_specs,
        out_shape=out_shape,
        scratch_shapes=[
            pltpu.VMEM((len(POOL_WINDOWS), tm, tm + 2 * POOL_HALO), BF16),
            pltpu.VMEM((tm + 2 * POOL_HALO, POOL_WIDTH), BF16),
            pltpu.VMEM((tm, POOL_WIDTH), BF16),
            pltpu.VMEM((tm, ATTN_WIDTH), BF16),
            pltpu.VMEM((tm, POOL_WIDTH), BF16),
            pltpu.VMEM((tm, ATTN_WIDTH), BF16),
            pltpu.VMEM((tm, D_MODEL), BF16),
            pltpu.VMEM((2 * HEADS_PER_GROUP, tm, HEAD_DIM), F32),
            pltpu.VMEM((2, tm, HEAD_DIM), F32),
            pltpu.VMEM((len(DILATIONS), tm, HEAD_DIM), F32),
            pltpu.VMEM((tm, HEAD_DIM), F32),
        ],
        compiler_params=pltpu.CompilerParams(
            dimension_semantics=("arbitrary",), vmem_limit_bytes=VMEM_LIMIT_BYTES),
        name="merge_final" if final else "merge",
    )(z, z, z, z, z, z, z, o0, o1, o2, l0, l1, l2, z, z, *x_args, mod_rows,
      w_grp, pool_scale, w_pp, w_pa, w_out, next_gain, mod_rows, mod_rows)


def kernel(x_prompt, x_sample, c_prompt, c_sample, norm_gain, w_ada, b_ada, w_in, w_pool_grp, pool_scale,
           w_proj_pool, w_proj_attn, w_out, final_gain):
    bp, sp, d = x_prompt.shape
    bs, ss, _ = x_sample.shape
    assert d == D_MODEL
    lay = (bp, sp, bs, ss)
    depth = w_in.shape[0]
    nseq = bp + bs
    gw = GROUP_WIDTH
    for s in (sp, ss):
        assert s % INPROJ_TM == 0 and s % MERGE_TM == 0 and s % (max(DILATIONS) * ATTN_SUB) == 0

    x = (x_prompt.reshape(bp * sp, d), x_sample.reshape(bs * ss, d))
    c = jnp.concatenate([c_prompt, c_sample], axis=0)
    mod_rows = _modulation(c, w_ada, b_ada).reshape(depth * nseq * 3, 1, d)
    cos, sin = _rope_tables(max(sp, ss))

    weights = (w_pool_grp.astype(BF16), pool_scale.reshape(depth, 1, POOL_WIDTH), (0.5 * w_proj_pool).astype(BF16),
               (0.5 * w_proj_attn).astype(BF16), w_out.astype(BF16))
    gains = norm_gain.reshape(depth, 1, d)

    h = _prenorm(x, norm_gain[0], mod_rows, lay)
    for layer in range(depth):
        z, qk1, qk2, v1, v2 = _in_proj(h, w_in, layer, cos, sin, lay)
        o0, l0 = _band_attention(z[None], z[None], ZC_Q0 // gw, ZC_K0 // gw, ZC_V0 // gw, DILATIONS[0], lay)
        attn = [
            (o0[0], l0[0]),
            _band_attention(qk1, v1, 0, 1, 0, DILATIONS[1], lay),
            _band_attention(qk2, v2, 0, 1, 0, DILATIONS[2], lay),
        ]
        if layer + 1 < depth:
            x, h = _merge(z, attn, x, mod_rows, layer, *weights, gains, layer + 1, lay)
        else:
            last = (z, attn, x, mod_rows, layer, *weights, final_gain.reshape(1, 1, d), 0, lay)
            y_prompt = _merge(*last, final=True, row_start=0, rows=bp * sp)
            y_sample = _merge(*last, final=True, row_start=bp * sp, rows=bs * ss)
    return (y_prompt.reshape(bp, sp, d), y_sample.reshape(bs, ss, d))
```

```python
import functools
import math

import jax
import jax.numpy as jnp
from jax import lax
from jax.experimental import pallas as pl
from jax.experimental.pallas import tpu as pltpu

F32 = jnp.float32
BF16 = jnp.bfloat16

D_MODEL = 2048
POOL_WIDTH = D_MODEL // 2
POOL_WINDOWS = (2, 4, 8, 16)
POOL_GROUP = POOL_WIDTH // len(POOL_WINDOWS)
POOL_HALO = 64
assert POOL_HALO >= max(POOL_WINDOWS) // 2
HEAD_DIM = 128
ATTN_GROUPS = ((128, 1), (512, 4), (2048, 16))
DILATIONS = tuple(dil for _, dil in ATTN_GROUPS)
HEADS_PER_GROUP = 4
GROUP_WIDTH = HEADS_PER_GROUP * HEAD_DIM
ATTN_WIDTH = GROUP_WIDTH * len(ATTN_GROUPS)
BAND_RADIUS = 64
assert all(win // (2 * dil) == BAND_RADIUS for win, dil in ATTN_GROUPS)
assert DILATIONS[0] == 1 and len(DILATIONS) == 3
ROPE_THETA = 10000.0
NORM_EPS = 1e-6
NEG_BIG = -1e30
IN_WIDTH = 2 * POOL_WIDTH + 4 * ATTN_WIDTH + 2 * D_MODEL

_CH_POOL_IN, _CH_POOL_GATE, _CH_Q, _CH_K, _CH_V, _CH_AG, _CH_GP, _CH_GA = 0, 2, 4, 7, 10, 13, 16, 20
IN_PROJ_CHUNK_ORDER = (
    _CH_Q, _CH_K, _CH_Q + 1, _CH_K + 1, _CH_Q + 2, _CH_K + 2, _CH_V + 1, _CH_V + 2,
    _CH_V, _CH_AG, *range(_CH_GP, _CH_GP + 4), *range(_CH_GA, _CH_GA + 4),
    _CH_POOL_IN, _CH_POOL_IN + 1, _CH_POOL_GATE, _CH_POOL_GATE + 1, _CH_AG + 1, _CH_AG + 2)
assert sorted(IN_PROJ_CHUNK_ORDER) == list(range(IN_WIDTH // GROUP_WIDTH))
HALVED_CHUNKS = frozenset((*range(_CH_POOL_GATE, _CH_POOL_GATE + 2), *range(_CH_AG, _CH_AG + 3),
                           *range(_CH_GP, _CH_GP + 4), *range(_CH_GA, _CH_GA + 4)))
IN_PROJ_CHUNK_SCALE = tuple(0.5 if c in HALVED_CHUNKS else 1.0 for c in IN_PROJ_CHUNK_ORDER)
N_SPECIAL_TILES = 4
Z_WIDTH = IN_WIDTH - 6 * GROUP_WIDTH
ZC_Q0, ZC_K0, ZC_V0, ZC_AG0 = 0, GROUP_WIDTH, 2 * GROUP_WIDTH, 3 * GROUP_WIDTH
ZC_G_POOL = 4 * GROUP_WIDTH
ZC_G_ATTN = ZC_G_POOL + D_MODEL
ZC_POOL_IN = ZC_G_ATTN + D_MODEL
ZC_POOL_GATE = ZC_POOL_IN + POOL_WIDTH
ZC_AG1 = ZC_POOL_GATE + POOL_WIDTH
ZC_AG2 = ZC_AG1 + GROUP_WIDTH

VMEM_LIMIT_BYTES = 60 * 1024 * 1024

INPROJ_TM = 1024
INPROJ_TN = 2 * GROUP_WIDTH
INPROJ_PIECE = 256
ATTN_BQ = 768
ATTN_SUB = 128
MERGE_TM = 256
MERGE_PIECE = 256
NORM_TM = 512


def _one_plus_tanh(x_half):
    return 1.0 + jnp.tanh(x_half)


def _mod_norm(x, gain, scale, shift):
    xn = x * lax.rsqrt(jnp.mean(x * x, axis=-1, keepdims=True) + NORM_EPS) * gain
    return xn * (1.0 + scale) + shift


def _locate(row0, lay, dil=1):
    bp, sp, bs, ss = lay
    rows_p, lp, ls = bp * sp // dil, sp // dil, ss // dil
    in_p = row0 < rows_p
    seq = jnp.where(in_p, row0 // lp, bp + (row0 - rows_p) // ls)
    local = jnp.where(in_p, row0 % lp, (row0 - rows_p) % ls)
    length = jnp.where(in_p, lp, ls)
    return seq, local, length


def _select_by_tile(j, table):
    out = table[0]
    for k in range(1, len(table)):
        out = jnp.where(j == k, table[k], out)
    return out


def _mod_spec(layer, which, tm, lay, tile=lambda i: i):
    nseq = lay[0] + lay[2]

    def idx(i):
        seq, _, _ = _locate(tile(i) * tm, lay)
        return ((layer * nseq + seq) * 3 + which, 0, 0)

    return pl.BlockSpec((1, 1, D_MODEL), idx)


def _stream_specs(tm, lay, tile=lambda i: i):
    tiles_p = lay[0] * lay[1] // tm
    tiles_s = lay[2] * lay[3] // tm
    return [
        pl.BlockSpec((tm, D_MODEL), lambda i: (jnp.minimum(tile(i), tiles_p - 1), 0)),
        pl.BlockSpec((tm, D_MODEL), lambda i: (jnp.clip(tile(i) - tiles_p, 0, tiles_s - 1), 0)),
    ]


def _stream_tile(xp_ref, xs_ref, tile, tm, lay):
    return jnp.where(tile * tm < lay[0] * lay[1], xp_ref[...], xs_ref[...])


def _mod_kernel(c_ref, w_ref, b_ref, o_ref):
    c = c_ref[...].astype(BF16)
    w = w_ref[0].astype(BF16)
    o_ref[0] = jnp.dot(c, w, preferred_element_type=F32) + b_ref[0]


def _modulation(c, w_ada, b_ada):
    depth, d, n = w_ada.shape
    nseq = c.shape[0]
    tn = 1536
    return pl.pallas_call(
        _mod_kernel,
        grid=(depth, n // tn),
        in_specs=[
            pl.BlockSpec((nseq, d), lambda l, j: (0, 0)),
            pl.BlockSpec((1, d, tn), lambda l, j: (l, 0, j)),
            pl.BlockSpec((1, 1, tn), lambda l, j: (l, 0, j)),
        ],
        out_specs=pl.BlockSpec((1, nseq, tn), lambda l, j: (l, 0, j)),
        out_shape=jax.ShapeDtypeStruct((depth, nseq, n), F32),
        compiler_params=pltpu.CompilerParams(
            dimension_semantics=("arbitrary", "arbitrary"), vmem_limit_bytes=VMEM_LIMIT_BYTES),
        name="modulation",
    )(c, w_ada, b_ada.reshape(depth, 1, n))


def _rope_table_kernel(cos_ref, sin_ref):
    s = cos_ref.shape[0]
    half = HEAD_DIM // 2
    pos = lax.broadcasted_iota(jnp.int32, (s, HEAD_DIM), 0).astype(F32)
    lane = lax.broadcasted_iota(jnp.int32, (s, HEAD_DIM), 1)
    idx = jnp.where(lane < half, lane, lane - half).astype(F32)
    inv = jnp.exp(idx * (-math.log(ROPE_THETA) / half))
    ang = pos * inv
    cos_ref[...] = jnp.cos(ang)
    sin_ref[...] = jnp.where(lane < half, -jnp.sin(ang), jnp.sin(ang))


def _rope_tables(s):
    return pl.pallas_call(
        _rope_table_kernel,
        out_shape=(jax.ShapeDtypeStruct((s, HEAD_DIM), F32), jax.ShapeDtypeStruct((s, HEAD_DIM), F32)),
        name="rope_tables",
    )()


def _prenorm_kernel(xp_ref, xs_ref, g_ref, shift_ref, scale_ref, h_ref, *, lay):
    tm = h_ref.shape[0]
    x = _stream_tile(xp_ref, xs_ref, pl.program_id(0), tm, lay)
    h_ref[...] = _mod_norm(x, g_ref[...], scale_ref[0], shift_ref[0]).astype(BF16)


def _prenorm(x_streams, gain, mod_rows, lay):
    d = D_MODEL
    t = lay[0] * lay[1] + lay[2] * lay[3]
    tm = NORM_TM
    return pl.pallas_call(
        functools.partial(_prenorm_kernel, lay=lay),
        grid=(t // tm,),
        in_specs=[
            *_stream_specs(tm, lay),
            pl.BlockSpec((1, d), lambda i: (0, 0)),
            _mod_spec(0, 0, tm, lay),
            _mod_spec(0, 1, tm, lay),
        ],
        out_specs=pl.BlockSpec((tm, d), lambda i: (i, 0)),
        out_shape=jax.ShapeDtypeStruct((t, d), BF16),
        compiler_params=pltpu.CompilerParams(
            dimension_semantics=("parallel",), vmem_limit_bytes=VMEM_LIMIT_BYTES),
        name="prenorm",
    )(*x_streams, gain.reshape(1, d), mod_rows, mod_rows)


def _inproj_kernel(chunk_tab_ref, h_ref, wa_ref, wb_ref, cos_ref, sin_ref, z_ref, qk1_ref, qk2_ref, v1_ref, v2_ref,
                   w_ref, s_ref, t_ref, *, lay):
    del chunk_tab_ref
    j = pl.program_id(0)
    n_chunks, tm, _ = s_ref.shape
    d1, d2 = DILATIONS[1], DILATIONS[2]
    hpg = HEADS_PER_GROUP
    per_piece = INPROJ_PIECE // HEAD_DIM

    @pl.when(pl.program_id(1) == 0)
    def _():
        for half, src_ref in enumerate((wa_ref, wb_ref)):
            w_ref[half] = (src_ref[...] * _select_by_tile(j, IN_PROJ_CHUNK_SCALE[half::2])).astype(BF16)

    def chunk(hh):
        return slice(hh * HEAD_DIM, (hh + 1) * HEAD_DIM)

    def pieces():
        pieces_per_half = GROUP_WIDTH // INPROJ_PIECE
        for c in range(n_chunks // per_piece):
            c_in = c % pieces_per_half
            cols = slice(c_in * INPROJ_PIECE, (c_in + 1) * INPROJ_PIECE)
            acc = jnp.dot(h_ref[...], w_ref[c // pieces_per_half, :, cols], preferred_element_type=F32)
            for k in range(per_piece):
                yield c * per_piece + k, acc[:, chunk(k)]

    def rope(xh):
        _, local, _ = _locate(pl.program_id(1) * tm, lay)
        rows = pl.ds(pl.multiple_of(local, tm), tm)
        return xh * cos_ref[rows, :] + pltpu.roll(xh, HEAD_DIM // 2, 1) * sin_ref[rows, :]

    def to_classes(dst_ref, dil, hh, c_out):
        if dil == d1 * d1:
            for r1 in range(d1):
                t_ref[hh % 2, r1] = s_ref[hh, pl.ds(r1, tm // d1, stride=d1), :]
            for r in range(dil):
                rows = pl.ds(r // d1, tm // dil, stride=d1)
                dst_ref[r, :, chunk(c_out)] = t_ref[hh % 2, r % d1, rows, :].astype(BF16)
        else:
            for r in range(dil):
                dst_ref[r, :, chunk(c_out)] = s_ref[hh, pl.ds(r, tm // dil, stride=dil), :].astype(BF16)

    @pl.when(j == 0)
    def _():
        for hh, xh in pieces():
            z_ref[:, chunk(hh)] = rope(xh).astype(BF16)

    @pl.when(j == 1)
    def _():
        for hh, xh in pieces():
            s_ref[hh] = rope(xh)
            to_classes(qk1_ref, d1, hh, hh)

    @pl.when(j == 2)
    def _():
        for hh, xh in pieces():
            s_ref[hh] = rope(xh)
            to_classes(qk2_ref, d2, hh, hh)

    @pl.when(j == 3)
    def _():
        for hh, xh in pieces():
            s_ref[hh] = xh
            if hh < hpg:
                to_classes(v1_ref, d1, hh, hh)
            else:
                to_classes(v2_ref, d2, hh, hh - hpg)

    @pl.when(j >= N_SPECIAL_TILES)
    def _():
        for hh, xh in pieces():
            z_ref[:, chunk(hh)] = xh.astype(BF16)


def _in_proj(h, w_in, layer, cos, sin, lay):
    t, d = h.shape
    n = w_in.shape[2]
    tm, tn, gw = INPROJ_TM, INPROJ_TN, GROUP_WIDTH
    d1, d2 = DILATIONS[1], DILATIONS[2]
    last = t // tm - 1

    def z_idx(j, i, tab):
        special = (j >= 1) & (j < N_SPECIAL_TILES)
        return (jnp.where(special, last, i), jnp.maximum(j - (N_SPECIAL_TILES - 1), 0))

    def cls_spec(dil, width, written_at):
        return pl.BlockSpec(
            (dil, tm // dil, width),
            lambda j, i, tab: (0, jnp.where(j < written_at, 0, jnp.where(j == written_at, i, last)), 0))

    def cls_shape(dil, width):
        return jax.ShapeDtypeStruct((dil, t // dil, width), BF16)

    rope_table_spec = pl.BlockSpec(cos.shape, lambda j, i, tab: (0, 0), pipeline_mode=pl.Buffered(1))

    def w_half_spec(half):
        return pl.BlockSpec((None, d, gw), lambda j, i, tab: (layer, 0, tab[2 * j + half]))

    grid_spec = pltpu.PrefetchScalarGridSpec(
        num_scalar_prefetch=1,
        grid=(n // tn, t // tm),
        in_specs=[
            pl.BlockSpec((tm, d), lambda j, i, tab: (i, 0)),
            w_half_spec(0),
            w_half_spec(1),
            rope_table_spec,
            rope_table_spec,
        ],
        out_specs=[
            pl.BlockSpec((tm, tn), z_idx),
            cls_spec(d1, 2 * gw, 1), cls_spec(d2, 2 * gw, 2), cls_spec(d1, gw, 3), cls_spec(d2, gw, 3),
        ],
        scratch_shapes=[pltpu.VMEM((2, d, gw), BF16),
                        pltpu.VMEM((tn // HEAD_DIM, tm, HEAD_DIM), F32),
                        pltpu.VMEM((2, d1, tm // d1, HEAD_DIM), F32)],
    )
    return pl.pallas_call(
        functools.partial(_inproj_kernel, lay=lay),
        grid_spec=grid_spec,
        out_shape=[
            jax.ShapeDtypeStruct((t, Z_WIDTH), BF16),
            cls_shape(d1, 2 * gw), cls_shape(d2, 2 * gw), cls_shape(d1, gw), cls_shape(d2, gw),
        ],
        compiler_params=pltpu.CompilerParams(
            dimension_semantics=("arbitrary", "arbitrary"), vmem_limit_bytes=VMEM_LIMIT_BYTES),
        name="in_proj",
    )(jnp.asarray(IN_PROJ_CHUNK_ORDER, jnp.int32), h, w_in, w_in, cos, sin)


def _attn_kernel(q_ref, kp_ref, km_ref, kn_ref, vp_ref, vm_ref, vn_ref, o_ref, lse_ref, k_buf, v_buf, *, dil, lay):
    bq = q_ref.shape[0]
    r = BAND_RADIUS
    k_buf[0:r] = kp_ref[...]
    k_buf[r:r + bq] = km_ref[...]
    k_buf[r + bq:] = kn_ref[...]
    v_buf[0:r] = vp_ref[...]
    v_buf[r:r + bq] = vm_ref[...]
    v_buf[r + bq:] = vn_ref[...]

    nk = ATTN_SUB + 2 * r
    qi = lax.broadcasted_iota(jnp.int32, (ATTN_SUB, nk), 0)
    kj = lax.broadcasted_iota(jnp.int32, (ATTN_SUB, nk), 1) - r
    in_band = jnp.abs(kj - qi) <= r
    inv_sqrt = 1.0 / math.sqrt(HEAD_DIM)

    for sb in range(bq // ATTN_SUB):
        a = sb * ATTN_SUB
        _, local, length = _locate(pl.program_id(1) * bq + a, lay, dil)
        mask = in_band & (kj + local >= 0) & (kj + local < length)
        lse_ref[a:a + ATTN_SUB, :] = jnp.zeros((ATTN_SUB, HEAD_DIM), F32)
        for hh in range(HEADS_PER_GROUP):
            cs = slice(hh * HEAD_DIM, (hh + 1) * HEAD_DIM)
            q = q_ref[a:a + ATTN_SUB, cs]
            k = k_buf[a:a + nk, cs]
            v = v_buf[a:a + nk, cs]
            s = lax.dot_general(q, k, (((1,), (1,)), ((), ())), preferred_element_type=F32) * inv_sqrt
            s = jnp.where(mask, s, NEG_BIG)
            m = jnp.max(s, axis=-1, keepdims=True)
            p = jnp.exp(s - m)
            l = jnp.sum(p, axis=-1, keepdims=True)
            o = jnp.dot(p.astype(BF16), v, preferred_element_type=F32) * (1.0 / l)
            o_ref[a:a + ATTN_SUB, cs] = o.astype(BF16)
            lse_ref[a:a + ATTN_SUB, hh:hh + 1] = m + jnp.log(l)


def _band_attention(qk, v, q_col, k_col, v_col, dil, lay):
    rows = qk.shape[1]
    bq, r, gw = ATTN_BQ, BAND_RADIUS, GROUP_WIDTH
    assert rows % bq == 0 and bq % r == 0
    halo_last = rows // r - 1

    def main(col):
        return pl.BlockSpec((None, bq, gw), lambda c, n: (c, n, col))

    def prev(col):
        return pl.BlockSpec((None, r, gw), lambda c, n: (c, jnp.maximum(n * (bq // r) - 1, 0), col))

    def nxt(col):
        return pl.BlockSpec((None, r, gw), lambda c, n: (c, jnp.minimum((n + 1) * (bq // r), halo_last), col))

    return pl.pallas_call(
        functools.partial(_attn_kernel, dil=dil, lay=lay),
        grid=(dil, rows // bq),
        in_specs=[main(q_col), prev(k_col), main(k_col), nxt(k_col), prev(v_col), main(v_col), nxt(v_col)],
        out_specs=[
            pl.BlockSpec((None, bq, gw), lambda c, n: (c, n, 0)),
            pl.BlockSpec((None, bq, HEAD_DIM), lambda c, n: (c, n, 0)),
        ],
        out_shape=[
            jax.ShapeDtypeStruct((dil, rows, gw), BF16),
            jax.ShapeDtypeStruct((dil, rows, HEAD_DIM), F32),
        ],
        scratch_shapes=[pltpu.VMEM((bq + 2 * r, gw), BF16), pltpu.VMEM((bq + 2 * r, gw), BF16)],
        compiler_params=pltpu.CompilerParams(
            dimension_semantics=("parallel", "parallel"), vmem_limit_bytes=VMEM_LIMIT_BYTES),
        name=f"band_attn_d{dil}",
    )(qk, qk, qk, qk, v, v, v)


def _merge_kernel(*refs, lay, block_off, n_tiles, final, split_x):
    n_x = 2 if split_x else 1
    (pin_ref, pprev_ref, pnext_ref, pgate_ref, ag0_ref, ag1_ref, ag2_ref,
     o0_ref, o1_ref, o2_ref, l0_ref, l1_ref, l2_ref, gpool_ref, gattn_ref) = refs[:15]
    x_refs = refs[15:15 + n_x]
    (gate_ref, wgrp_ref, pscale_ref, wpp_ref, wpa_ref, wout_ref,
     ngain_ref, nshift_ref, nscale_ref) = refs[15 + n_x:24 + n_x]
    outs_and_scratch = refs[24 + n_x:]
    if final:
        y_ref, band_ref, ext_ref, a0_ref, b0_ref, a1_ref, b1_ref, m_ref, oi_ref, li_ref, wt_ref, sq_ref = outs_and_scratch
    else:
        out_ref, h_ref, band_ref, ext_ref, a0_ref, b0_ref, a1_ref, b1_ref, m_ref, oi_ref, li_ref, wt_ref, sq_ref = outs_and_scratch
    tm = gpool_ref.shape[0]
    hl = POOL_HALO
    gw = GROUP_WIDTH
    hpg = HEADS_PER_GROUP
    s = pl.program_id(0)

    @pl.when(s == 0)
    def _():
        i = lax.broadcasted_iota(jnp.int32, (tm, tm + 2 * hl), 0)
        e = lax.broadcasted_iota(jnp.int32, (tm, tm + 2 * hl), 1)
        off = e - hl - i
        for g, w in enumerate(POOL_WINDOWS):
            band_ref[g] = jnp.where((off >= -(w // 2)) & (off < w // 2), 1.0, 0.0).astype(BF16)
        a0_ref[...] = jnp.zeros_like(a0_ref)
        b0_ref[...] = jnp.zeros_like(b0_ref)

    def step(a_cur, b_cur, a_nxt, b_nxt):
        back_tile = jnp.maximum(s - 1, 0) + block_off
        front_tile = jnp.minimum(s, n_tiles - 1) + block_off
        _, t0, length = _locate(front_tile * tm, lay)
        n_pieces = D_MODEL // MERGE_PIECE

        def piece(c):
            return slice(c * MERGE_PIECE, (c + 1) * MERGE_PIECE)

        def head(hh):
            return slice(hh * HEAD_DIM, (hh + 1) * HEAD_DIM)

        def front_prepare():
            zeros = jnp.zeros((hl, POOL_WIDTH), BF16)
            ext_ref[0:hl] = jnp.where(t0 == 0, zeros, pprev_ref[...])
            ext_ref[hl:hl + tm] = pin_ref[...]
            ext_ref[hl + tm:] = jnp.where(t0 + tm == length, zeros, pnext_ref[...])
            for slot, (o_ref, l_ref, dil) in enumerate(((o1_ref, l1_ref, DILATIONS[1]), (o2_ref, l2_ref, DILATIONS[2]))):
                for r in range(dil):
                    rows = pl.ds(r, tm // dil, stride=dil)
                    for hh in range(hpg):
                        oi_ref[slot * hpg + hh, rows, :] = o_ref[r, :, head(hh)].astype(F32)
                    li_ref[slot, rows, :] = l_ref[r]
            lses = (l0_ref[...], li_ref[0], li_ref[1])
            mx = jnp.maximum(jnp.maximum(lses[0], lses[1]), lses[2])
            es = [jnp.exp(l - mx) for l in lses]
            inv_den = 1.0 / (es[0] + es[1] + es[2])
            for g in range(len(DILATIONS)):
                wt_ref[g] = es[g] * inv_den

        def front_head(g, hh):
            def run():
                ag_ref = (ag0_ref, ag1_ref, ag2_ref)[g]
                o = o0_ref[:, head(hh)].astype(F32) if g == 0 else oi_ref[(g - 1) * hpg + hh]
                gate_half = ag_ref[:, head(hh)].astype(F32)
                b = o * wt_ref[g, :, hh:hh + 1] * (gate_half * _one_plus_tanh(gate_half))
                b_nxt[:, g * gw + hh * HEAD_DIM:g * gw + (hh + 1) * HEAD_DIM] = b.astype(BF16)
            return run

        def front_pool(g):
            def run():
                w = POOL_WINDOWS[g]
                cs = slice(g * POOL_GROUP, (g + 1) * POOL_GROUP)
                t = t0 + lax.broadcasted_iota(jnp.int32, (tm, 1), 0)
                acc = jnp.dot(band_ref[g], ext_ref[:, cs], preferred_element_type=F32)
                cnt = (jnp.minimum(t + w // 2, length) - jnp.maximum(t - w // 2, 0)).astype(F32)
                p = acc * (1.0 / cnt) - pin_ref[:, cs].astype(F32)
                y = jnp.dot(p.astype(BF16), wgrp_ref[g], preferred_element_type=F32)
                gate_half = pgate_ref[:, cs].astype(F32)
                a = y * pscale_ref[:, cs] * (gate_half * _one_plus_tanh(gate_half))
                a_nxt[:, cs] = a.astype(BF16)
            return run

        front = [front_head(g, hh) for g in range(len(DILATIONS)) for hh in range(hpg)]
        front += [front_pool(g) for g in range(len(POOL_WINDOWS))]

        x_dst = y_ref if final else out_ref

        def back_merge(c):
            def run():
                pp = jnp.dot(a_cur[...], wpp_ref[:, piece(c)], preferred_element_type=F32)
                pa = jnp.dot(b_cur[...], wpa_ref[:, piece(c)], preferred_element_type=F32)
                merged = (_one_plus_tanh(gpool_ref[:, piece(c)].astype(F32)) * pp
                          + _one_plus_tanh(gattn_ref[:, piece(c)].astype(F32)) * pa)
                m_ref[:, piece(c)] = merged.astype(BF16)
            return run

        def back_out(c):
            def run():
                out = jnp.dot(m_ref[...], wout_ref[:, piece(c)], preferred_element_type=F32)
                if split_x:
                    x = jnp.where(back_tile * tm < lay[0] * lay[1], x_refs[0][:, piece(c)], x_refs[1][:, piece(c)])
                else:
                    x = x_refs[0][:, piece(c)]
                x_new = x + gate_ref[0, :, piece(c)] * out
                x_dst[:, piece(c)] = x_new
                sq_ref[:, c:c + 1] = jnp.sum(x_new * x_new, axis=-1, keepdims=True)
            return run

        back = [back_merge(c) for c in range(n_pieces)] + [back_out(c) for c in range(n_pieces)]

        front_prepare()
        for k in range(max(len(back), len(front))):
            if k < len(back):
                back[k]()
            if k < len(front):
                front[k]()

        sq = jnp.sum(sq_ref[:, 0:n_pieces], axis=-1, keepdims=True)
        inv_rms = lax.rsqrt(sq * (1.0 / D_MODEL) + NORM_EPS)
        for c in range(n_pieces):
            xn = x_dst[:, piece(c)] * inv_rms * ngain_ref[:, piece(c)]
            if final:
                y_ref[:, piece(c)] = xn
            else:
                h_ref[:, piece(c)] = (xn * (1.0 + nscale_ref[0, :, piece(c)]) + nshift_ref[0, :, piece(c)]).astype(BF16)

    @pl.when(s % 2 == 0)
    def _():
        step(a0_ref, b0_ref, a1_ref, b1_ref)

    @pl.when(s % 2 == 1)
    def _():
        step(a1_ref, b1_ref, a0_ref, b0_ref)


def _merge(z, attn, x, mod_rows, layer, w_grp, pool_scale, w_pp, w_pa, w_out, next_gain, gain_index, lay, *,
           final=False, row_start=0, rows=None):
    split_x = isinstance(x, (tuple, list))
    d = D_MODEL
    t = z.shape[0]
    rows = t if rows is None else rows
    tm = MERGE_TM
    hb = POOL_HALO
    assert row_start % tm == 0 and rows % tm == 0
    off = row_start // tm
    n_tiles = rows // tm
    halo_last = t // hb - 1
    (o0, l0), (o1, l1), (o2, l2) = attn
    d1, d2 = DILATIONS[1], DILATIONS[2]

    def front(s):
        return jnp.minimum(s, n_tiles - 1) + off

    def back(s):
        return jnp.maximum(s - 1, 0) + off

    def zcols(tile, width, col):
        assert col % width == 0
        return pl.BlockSpec((tm, width), lambda s: (tile(s), col // width))

    def const(stacked, index):
        tail = stacked.shape[1:]
        return pl.BlockSpec((None, *tail), lambda s: (index,) + (0,) * len(tail), pipeline_mode=pl.Buffered(1))

    def row(tile, width):
        return pl.BlockSpec((tm, width), lambda s: (tile(s), 0))

    def cls(dil, width):
        return pl.BlockSpec((dil, tm // dil, width), lambda s: (0, front(s), 0))

    next_layer = layer if final else layer + 1
    out_row = pl.BlockSpec((tm, d), lambda s: (back(s) - off, 0))
    if final:
        out_specs = out_row
        out_shape = jax.ShapeDtypeStruct((rows, d), F32)
    else:
        out_specs = [out_row, out_row]
        out_shape = [jax.ShapeDtypeStruct((rows, d), F32), jax.ShapeDtypeStruct((rows, d), BF16)]
    x_specs = _stream_specs(tm, lay, back) if split_x else [row(back, d)]
    x_args = tuple(x) if split_x else (x,)

    return pl.pallas_call(
        functools.partial(_merge_kernel, lay=lay, block_off=off, n_tiles=n_tiles, final=final, split_x=split_x),
        grid=(n_tiles + 1,),
        in_specs=[
            zcols(front, POOL_WIDTH, ZC_POOL_IN),
            pl.BlockSpec((hb, POOL_WIDTH),
                         lambda s: (jnp.maximum(front(s) * (tm // hb) - 1, 0), ZC_POOL_IN // POOL_WIDTH)),
            pl.BlockSpec((hb, POOL_WIDTH),
                         lambda s: (jnp.minimum((front(s) + 1) * (tm // hb), halo_last), ZC_POOL_IN // POOL_WIDTH)),
            zcols(front, POOL_WIDTH, ZC_POOL_GATE),
            zcols(front, GROUP_WIDTH, ZC_AG0),
            zcols(front, GROUP_WIDTH, ZC_AG1),
            zcols(front, GROUP_WIDTH, ZC_AG2),
            row(front, GROUP_WIDTH), cls(d1, GROUP_WIDTH), cls(d2, GROUP_WIDTH),
            row(front, HEAD_DIM), cls(d1, HEAD_DIM), cls(d2, HEAD_DIM),
            zcols(back, D_MODEL, ZC_G_POOL),
            zcols(back, D_MODEL, ZC_G_ATTN),
            *x_specs,
            _mod_spec(layer, 2, tm, lay, back),
            const(w_grp, layer), const(pool_scale, layer), const(w_pp, layer), const(w_pa, layer),
            const(w_out, layer), const(next_gain, gain_index),
            _mod_spec(next_layer, 0, tm, lay, back),
            _mod_spec(next_layer, 1, tm, lay, back),
        ],
        out_specs=out_specs,
        out_shape=out_shape,
        scratch_shapes=[
            pltpu.VMEM((len(POOL_WINDOWS), tm, tm + 2 * POOL_HALO), BF16),
            pltpu.VMEM((tm + 2 * POOL_HALO, POOL_WIDTH), BF16),
            pltpu.VMEM((tm, POOL_WIDTH), BF16),
            pltpu.VMEM((tm, ATTN_WIDTH), BF16),
            pltpu.VMEM((tm, POOL_WIDTH), BF16),
            pltpu.VMEM((tm, ATTN_WIDTH), BF16),
            pltpu.VMEM((tm, D_MODEL), BF16),
            pltpu.VMEM((2 * HEADS_PER_GROUP, tm, HEAD_DIM), F32),
            pltpu.VMEM((2, tm, HEAD_DIM), F32),
            pltpu.VMEM((len(DILATIONS), tm, HEAD_DIM), F32),
            pltpu.VMEM((tm, HEAD_DIM), F32),
        ],
        compiler_params=pltpu.CompilerParams(
            dimension_semantics=("arbitrary",), vmem_limit_bytes=VMEM_LIMIT_BYTES),
        name="merge_final" if final else "merge",
    )(z, z, z, z, z, z, z, o0, o1, o2, l0, l1, l2, z, z, *x_args, mod_rows,
      w_grp, pool_scale, w_pp, w_pa, w_out, next_gain, mod_rows, mod_rows)


def kernel(x_prompt, x_sample, c_prompt, c_sample, norm_gain, w_ada, b_ada, w_in, w_pool_grp, pool_scale,
           w_proj_pool, w_proj_attn, w_out, final_gain):
    bp, sp, d = x_prompt.shape
    bs, ss, _ = x_sample.shape
    assert d == D_MODEL
    lay = (bp, sp, bs, ss)
    depth = w_in.shape[0]
    nseq = bp + bs
    gw = GROUP_WIDTH
    for s in (sp, ss):
        assert s % INPROJ_TM == 0 and s % MERGE_TM == 0 and s % (max(DILATIONS) * ATTN_SUB) == 0

    x = (x_prompt.reshape(bp * sp, d), x_sample.reshape(bs * ss, d))
    c = jnp.concatenate([c_prompt, c_sample], axis=0)
    mod_rows = _modulation(c, w_ada, b_ada).reshape(depth * nseq * 3, 1, d)
    cos, sin = _rope_tables(max(sp, ss))

    weights = (w_pool_grp.astype(BF16), pool_scale.reshape(depth, 1, POOL_WIDTH), (0.5 * w_proj_pool).astype(BF16),
               (0.5 * w_proj_attn).astype(BF16), w_out.astype(BF16))
    gains = norm_gain.reshape(depth, 1, d)

    h = _prenorm(x, norm_gain[0], mod_rows, lay)
    for layer in range(depth):
        z, qk1, qk2, v1, v2 = _in_proj(h, w_in, layer, cos, sin, lay)
        o0, l0 = _band_attention(z[None], z[None], ZC_Q0 // gw, ZC_K0 // gw, ZC_V0 // gw, DILATIONS[0], lay)
        attn = [
            (o0[0], l0[0]),
            _band_attention(qk1, v1, 0, 1, 0, DILATIONS[1], lay),
            _band_attention(qk2, v2, 0, 1, 0, DILATIONS[2], lay),
        ]
        if layer + 1 < depth:
            x, h = _merge(z, attn, x, mod_rows, layer, *weights, gains, layer + 1, lay)
        else:
            last = (z, attn, x, mod_rows, layer, *weights, final_gain.reshape(1, 1, d), 0, lay)
            y_prompt = _merge(*last, final=True, row_start=0, rows=bp * sp)
            y_sample = _merge(*last, final=True, row_start=bp * sp, rows=bs * ss)
    return (y_prompt.reshape(bp, sp, d), y_sample.reshape(bs, ss, d))
```

```python
import functools
import math

import jax
import jax.numpy as jnp
from jax import lax
from jax.experimental import pallas as pl
from jax.experimental.pallas import tpu as pltpu

F32 = jnp.float32
BF16 = jnp.bfloat16

D_MODEL = 2048
POOL_WIDTH = D_MODEL // 2
POOL_WINDOWS = (2, 4, 8, 16)
POOL_GROUP = POOL_WIDTH // len(POOL_WINDOWS)
POOL_HALO = 64
assert POOL_HALO >= max(POOL_WINDOWS) // 2
HEAD_DIM = 128
ATTN_GROUPS = ((128, 1), (512, 4), (2048, 16))
DILATIONS = tuple(dil for _, dil in ATTN_GROUPS)
HEADS_PER_GROUP = 4
GROUP_WIDTH = HEADS_PER_GROUP * HEAD_DIM
ATTN_WIDTH = GROUP_WIDTH * len(ATTN_GROUPS)
BAND_RADIUS = 64
assert all(win // (2 * dil) == BAND_RADIUS for win, dil in ATTN_GROUPS)
assert DILATIONS[0] == 1 and len(DILATIONS) == 3
ROPE_THETA = 10000.0
NORM_EPS = 1e-6
NEG_BIG = -1e30
IN_WIDTH = 2 * POOL_WIDTH + 4 * ATTN_WIDTH + 2 * D_MODEL

_CH_POOL_IN, _CH_POOL_GATE, _CH_Q, _CH_K, _CH_V, _CH_AG, _CH_GP, _CH_GA = 0, 2, 4, 7, 10, 13, 16, 20
IN_PROJ_CHUNK_ORDER = (
    _CH_Q, _CH_K, _CH_Q + 1, _CH_K + 1, _CH_Q + 2, _CH_K + 2, _CH_V + 1, _CH_V + 2,
    _CH_V, _CH_AG, *range(_CH_GP, _CH_GP + 4), *range(_CH_GA, _CH_GA + 4),
    _CH_POOL_IN, _CH_POOL_IN + 1, _CH_POOL_GATE, _CH_POOL_GATE + 1, _CH_AG + 1, _CH_AG + 2)
assert sorted(IN_PROJ_CHUNK_ORDER) == list(range(IN_WIDTH // GROUP_WIDTH))
HALVED_CHUNKS = frozenset((*range(_CH_POOL_GATE, _CH_POOL_GATE + 2), *range(_CH_AG, _CH_AG + 3),
                           *range(_CH_GP, _CH_GP + 4), *range(_CH_GA, _CH_GA + 4)))
IN_PROJ_CHUNK_SCALE = tuple(0.5 if c in HALVED_CHUNKS else 1.0 for c in IN_PROJ_CHUNK_ORDER)
N_SPECIAL_TILES = 4
Z_WIDTH = IN_WIDTH - 6 * GROUP_WIDTH
ZC_Q0, ZC_K0, ZC_V0, ZC_AG0 = 0, GROUP_WIDTH, 2 * GROUP_WIDTH, 3 * GROUP_WIDTH
ZC_G_POOL = 4 * GROUP_WIDTH
ZC_G_ATTN = ZC_G_POOL + D_MODEL
ZC_POOL_IN = ZC_G_ATTN + D_MODEL
ZC_POOL_GATE = ZC_POOL_IN + POOL_WIDTH
ZC_AG1 = ZC_POOL_GATE + POOL_WIDTH
ZC_AG2 = ZC_AG1 + GROUP_WIDTH

VMEM_LIMIT_BYTES = 60 * 1024 * 1024

INPROJ_TM = 1024
INPROJ_TN = 2 * GROUP_WIDTH
INPROJ_PIECE = 256
ATTN_BQ = 1536
ATTN_SUB = 128
MERGE_TM = 256
MERGE_PIECE = 256
NORM_TM = 512


def _one_plus_tanh(x_half):
    return 1.0 + jnp.tanh(x_half)


def _mod_norm(x, gain, scale, shift):
    xn = x * lax.rsqrt(jnp.mean(x * x, axis=-1, keepdims=True) + NORM_EPS) * gain
    return xn * (1.0 + scale) + shift


def _locate(row0, lay, dil=1):
    bp, sp, bs, ss = lay
    rows_p, lp, ls = bp * sp // dil, sp // dil, ss // dil
    in_p = row0 < rows_p
    seq = jnp.where(in_p, row0 // lp, bp + (row0 - rows_p) // ls)
    local = jnp.where(in_p, row0 % lp, (row0 - rows_p) % ls)
    length = jnp.where(in_p, lp, ls)
    return seq, local, length


def _select_by_tile(j, table):
    out = table[0]
    for k in range(1, len(table)):
        out = jnp.where(j == k, table[k], out)
    return out


def _mod_spec(layer, which, tm, lay, tile=lambda i: i):
    nseq = lay[0] + lay[2]

    def idx(i):
        seq, _, _ = _locate(tile(i) * tm, lay)
        return ((layer * nseq + seq) * 3 + which, 0, 0)

    return pl.BlockSpec((1, 1, D_MODEL), idx)


def _stream_specs(tm, lay, tile=lambda i: i):
    tiles_p = lay[0] * lay[1] // tm
    tiles_s = lay[2] * lay[3] // tm
    return [
        pl.BlockSpec((tm, D_MODEL), lambda i: (jnp.minimum(tile(i), tiles_p - 1), 0)),
        pl.BlockSpec((tm, D_MODEL), lambda i: (jnp.clip(tile(i) - tiles_p, 0, tiles_s - 1), 0)),
    ]


def _stream_tile(xp_ref, xs_ref, tile, tm, lay):
    return jnp.where(tile * tm < lay[0] * lay[1], xp_ref[...], xs_ref[...])


def _mod_kernel(c_ref, w_ref, b_ref, o_ref):
    c = c_ref[...].astype(BF16)
    w = w_ref[0].astype(BF16)
    o_ref[0] = jnp.dot(c, w, preferred_element_type=F32) + b_ref[0]


def _modulation(c, w_ada, b_ada):
    depth, d, n = w_ada.shape
    nseq = c.shape[0]
    tn = 1536
    return pl.pallas_call(
        _mod_kernel,
        grid=(depth, n // tn),
        in_specs=[
            pl.BlockSpec((nseq, d), lambda l, j: (0, 0)),
            pl.BlockSpec((1, d, tn), lambda l, j: (l, 0, j)),
            pl.BlockSpec((1, 1, tn), lambda l, j: (l, 0, j)),
        ],
        out_specs=pl.BlockSpec((1, nseq, tn), lambda l, j: (l, 0, j)),
        out_shape=jax.ShapeDtypeStruct((depth, nseq, n), F32),
        compiler_params=pltpu.CompilerParams(
            dimension_semantics=("arbitrary", "arbitrary"), vmem_limit_bytes=VMEM_LIMIT_BYTES),
        name="modulation",
    )(c, w_ada, b_ada.reshape(depth, 1, n))


def _rope_table_kernel(cos_ref, sin_ref):
    s = cos_ref.shape[0]
    half = HEAD_DIM // 2
    pos = lax.broadcasted_iota(jnp.int32, (s, HEAD_DIM), 0).astype(F32)
    lane = lax.broadcasted_iota(jnp.int32, (s, HEAD_DIM), 1)
    idx = jnp.where(lane < half, lane, lane - half).astype(F32)
    inv = jnp.exp(idx * (-math.log(ROPE_THETA) / half))
    ang = pos * inv
    cos_ref[...] = jnp.cos(ang)
    sin_ref[...] = jnp.where(lane < half, -jnp.sin(ang), jnp.sin(ang))


def _rope_tables(s):
    return pl.pallas_call(
        _rope_table_kernel,
        out_shape=(jax.ShapeDtypeStruct((s, HEAD_DIM), F32), jax.ShapeDtypeStruct((s, HEAD_DIM), F32)),
        name="rope_tables",
    )()


def _prenorm_kernel(xp_ref, xs_ref, g_ref, shift_ref, scale_ref, h_ref, *, lay):
    tm = h_ref.shape[0]
    x = _stream_tile(xp_ref, xs_ref, pl.program_id(0), tm, lay)
    h_ref[...] = _mod_norm(x, g_ref[...], scale_ref[0], shift_ref[0]).astype(BF16)


def _prenorm(x_streams, gain, mod_rows, lay):
    d = D_MODEL
    t = lay[0] * lay[1] + lay[2] * lay[3]
    tm = NORM_TM
    return pl.pallas_call(
        functools.partial(_prenorm_kernel, lay=lay),
        grid=(t // tm,),
        in_specs=[
            *_stream_specs(tm, lay),
            pl.BlockSpec((1, d), lambda i: (0, 0)),
            _mod_spec(0, 0, tm, lay),
            _mod_spec(0, 1, tm, lay),
        ],
        out_specs=pl.BlockSpec((tm, d), lambda i: (i, 0)),
        out_shape=jax.ShapeDtypeStruct((t, d), BF16),
        compiler_params=pltpu.CompilerParams(
            dimension_semantics=("parallel",), vmem_limit_bytes=VMEM_LIMIT_BYTES),
        name="prenorm",
    )(*x_streams, gain.reshape(1, d), mod_rows, mod_rows)


def _inproj_kernel(chunk_tab_ref, h_ref, wa_ref, wb_ref, cos_ref, sin_ref, z_ref, qk1_ref, qk2_ref, v1_ref, v2_ref,
                   w_ref, s_ref, t_ref, *, lay):
    del chunk_tab_ref
    j = pl.program_id(0)
    n_chunks, tm, _ = s_ref.shape
    d1, d2 = DILATIONS[1], DILATIONS[2]
    hpg = HEADS_PER_GROUP
    per_piece = INPROJ_PIECE // HEAD_DIM

    @pl.when(pl.program_id(1) == 0)
    def _():
        for half, src_ref in enumerate((wa_ref, wb_ref)):
            w_ref[half] = (src_ref[...] * _select_by_tile(j, IN_PROJ_CHUNK_SCALE[half::2])).astype(BF16)

    def chunk(hh):
        return slice(hh * HEAD_DIM, (hh + 1) * HEAD_DIM)

    def pieces():
        pieces_per_half = GROUP_WIDTH // INPROJ_PIECE
        for c in range(n_chunks // per_piece):
            c_in = c % pieces_per_half
            cols = slice(c_in * INPROJ_PIECE, (c_in + 1) * INPROJ_PIECE)
            acc = jnp.dot(h_ref[...], w_ref[c // pieces_per_half, :, cols], preferred_element_type=F32)
            for k in range(per_piece):
                yield c * per_piece + k, acc[:, chunk(k)]

    def rope(xh):
        _, local, _ = _locate(pl.program_id(1) * tm, lay)
        rows = pl.ds(pl.multiple_of(local, tm), tm)
        return xh * cos_ref[rows, :] + pltpu.roll(xh, HEAD_DIM // 2, 1) * sin_ref[rows, :]

    def to_classes(dst_ref, dil, hh, c_out):
        if dil == d1 * d1:
            for r1 in range(d1):
                t_ref[hh % 2, r1] = s_ref[hh, pl.ds(r1, tm // d1, stride=d1), :]
            for r in range(dil):
                rows = pl.ds(r // d1, tm // dil, stride=d1)
                dst_ref[r, :, chunk(c_out)] = t_ref[hh % 2, r % d1, rows, :].astype(BF16)
        else:
            for r in range(dil):
                dst_ref[r, :, chunk(c_out)] = s_ref[hh, pl.ds(r, tm // dil, stride=dil), :].astype(BF16)

    @pl.when(j == 0)
    def _():
        for hh, xh in pieces():
            z_ref[:, chunk(hh)] = rope(xh).astype(BF16)

    @pl.when(j == 1)
    def _():
        for hh, xh in pieces():
            s_ref[hh] = rope(xh)
            to_classes(qk1_ref, d1, hh, hh)

    @pl.when(j == 2)
    def _():
        for hh, xh in pieces():
            s_ref[hh] = rope(xh)
            to_classes(qk2_ref, d2, hh, hh)

    @pl.when(j == 3)
    def _():
        for hh, xh in pieces():
            s_ref[hh] = xh
            if hh < hpg:
                to_classes(v1_ref, d1, hh, hh)
            else:
                to_classes(v2_ref, d2, hh, hh - hpg)

    @pl.when(j >= N_SPECIAL_TILES)
    def _():
        for hh, xh in pieces():
            z_ref[:, chunk(hh)] = xh.astype(BF16)


def _in_proj(h, w_in, layer, cos, sin, lay):
    t, d = h.shape
    n = w_in.shape[2]
    tm, tn, gw = INPROJ_TM, INPROJ_TN, GROUP_WIDTH
    d1, d2 = DILATIONS[1], DILATIONS[2]
    last = t // tm - 1

    def z_idx(j, i, tab):
        special = (j >= 1) & (j < N_SPECIAL_TILES)
        return (jnp.where(special, last, i), jnp.maximum(j - (N_SPECIAL_TILES - 1), 0))

    def cls_spec(dil, width, written_at):
        return pl.BlockSpec(
            (dil, tm // dil, width),
            lambda j, i, tab: (0, jnp.where(j < written_at, 0, jnp.where(j == written_at, i, last)), 0))

    def cls_shape(dil, width):
        return jax.ShapeDtypeStruct((dil, t // dil, width), BF16)

    rope_table_spec = pl.BlockSpec(cos.shape, lambda j, i, tab: (0, 0), pipeline_mode=pl.Buffered(1))

    def w_half_spec(half):
        return pl.BlockSpec((None, d, gw), lambda j, i, tab: (layer, 0, tab[2 * j + half]))

    grid_spec = pltpu.PrefetchScalarGridSpec(
        num_scalar_prefetch=1,
        grid=(n // tn, t // tm),
        in_specs=[
            pl.BlockSpec((tm, d), lambda j, i, tab: (i, 0)),
            w_half_spec(0),
            w_half_spec(1),
            rope_table_spec,
            rope_table_spec,
        ],
        out_specs=[
            pl.BlockSpec((tm, tn), z_idx),
            cls_spec(d1, 2 * gw, 1), cls_spec(d2, 2 * gw, 2), cls_spec(d1, gw, 3), cls_spec(d2, gw, 3),
        ],
        scratch_shapes=[pltpu.VMEM((2, d, gw), BF16),
                        pltpu.VMEM((tn // HEAD_DIM, tm, HEAD_DIM), F32),
                        pltpu.VMEM((2, d1, tm // d1, HEAD_DIM), F32)],
    )
    return pl.pallas_call(
        functools.partial(_inproj_kernel, lay=lay),
        grid_spec=grid_spec,
        out_shape=[
            jax.ShapeDtypeStruct((t, Z_WIDTH), BF16),
            cls_shape(d1, 2 * gw), cls_shape(d2, 2 * gw), cls_shape(d1, gw), cls_shape(d2, gw),
        ],
        compiler_params=pltpu.CompilerParams(
            dimension_semantics=("arbitrary", "arbitrary"), vmem_limit_bytes=VMEM_LIMIT_BYTES),
        name="in_proj",
    )(jnp.asarray(IN_PROJ_CHUNK_ORDER, jnp.int32), h, w_in, w_in, cos, sin)


def _attn_kernel(q_ref, kp_ref, km_ref, kn_ref, vp_ref, vm_ref, vn_ref, o_ref, lse_ref, k_buf, v_buf, *, dil, lay):
    bq = q_ref.shape[0]
    r = BAND_RADIUS
    k_buf[0:r] = kp_ref[...]
    k_buf[r:r + bq] = km_ref[...]
    k_buf[r + bq:] = kn_ref[...]
    v_buf[0:r] = vp_ref[...]
    v_buf[r:r + bq] = vm_ref[...]
    v_buf[r + bq:] = vn_ref[...]

    nk = ATTN_SUB + 2 * r
    qi = lax.broadcasted_iota(jnp.int32, (ATTN_SUB, nk), 0)
    kj = lax.broadcasted_iota(jnp.int32, (ATTN_SUB, nk), 1) - r
    in_band = jnp.abs(kj - qi) <= r
    inv_sqrt = 1.0 / math.sqrt(HEAD_DIM)

    for sb in range(bq // ATTN_SUB):
        a = sb * ATTN_SUB
        _, local, length = _locate(pl.program_id(1) * bq + a, lay, dil)
        mask = in_band & (kj + local >= 0) & (kj + local < length)
        lse_ref[a:a + ATTN_SUB, :] = jnp.zeros((ATTN_SUB, HEAD_DIM), F32)
        for hh in range(HEADS_PER_GROUP):
            cs = slice(hh * HEAD_DIM, (hh + 1) * HEAD_DIM)
            q = q_ref[a:a + ATTN_SUB, cs]
            k = k_buf[a:a + nk, cs]
            v = v_buf[a:a + nk, cs]
            s = lax.dot_general(q, k, (((1,), (1,)), ((), ())), preferred_element_type=F32) * inv_sqrt
            s = jnp.where(mask, s, NEG_BIG)
            m = jnp.max(s, axis=-1, keepdims=True)
            p = jnp.exp(s - m)
            l = jnp.sum(p, axis=-1, keepdims=True)
            o = jnp.dot(p.astype(BF16), v, preferred_element_type=F32) * (1.0 / l)
            o_ref[a:a + ATTN_SUB, cs] = o.astype(BF16)
            lse_ref[a:a + ATTN_SUB, hh:hh + 1] = m + jnp.log(l)


def _band_attention(qk, v, q_col, k_col, v_col, dil, lay):
    rows = qk.shape[1]
    bq, r, gw = ATTN_BQ, BAND_RADIUS, GROUP_WIDTH
    assert rows % bq == 0 and bq % r == 0
    halo_last = rows // r - 1

    def main(col):
        return pl.BlockSpec((None, bq, gw), lambda c, n: (c, n, col))

    def prev(col):
        return pl.BlockSpec((None, r, gw), lambda c, n: (c, jnp.maximum(n * (bq // r) - 1, 0), col))

    def nxt(col):
        return pl.BlockSpec((None, r, gw), lambda c, n: (c, jnp.minimum((n + 1) * (bq // r), halo_last), col))

    return pl.pallas_call(
        functools.partial(_attn_kernel, dil=dil, lay=lay),
        grid=(dil, rows // bq),
        in_specs=[main(q_col), prev(k_col), main(k_col), nxt(k_col), prev(v_col), main(v_col), nxt(v_col)],
        out_specs=[
            pl.BlockSpec((None, bq, gw), lambda c, n: (c, n, 0)),
            pl.BlockSpec((None, bq, HEAD_DIM), lambda c, n: (c, n, 0)),
        ],
        out_shape=[
            jax.ShapeDtypeStruct((dil, rows, gw), BF16),
            jax.ShapeDtypeStruct((dil, rows, HEAD_DIM), F32),
        ],
        scratch_shapes=[pltpu.VMEM((bq + 2 * r, gw), BF16), pltpu.VMEM((bq + 2 * r, gw), BF16)],
        compiler_params=pltpu.CompilerParams(
            dimension_semantics=("parallel", "parallel"), vmem_limit_bytes=VMEM_LIMIT_BYTES),
        name=f"band_attn_d{dil}",
    )(qk, qk, qk, qk, v, v, v)


def _merge_kernel(*refs, lay, block_off, n_tiles, final, split_x):
    n_x = 2 if split_x else 1
    (pin_ref, pprev_ref, pnext_ref, pgate_ref, ag0_ref, ag1_ref, ag2_ref,
     o0_ref, o1_ref, o2_ref, l0_ref, l1_ref, l2_ref, gpool_ref, gattn_ref) = refs[:15]
    x_refs = refs[15:15 + n_x]
    (gate_ref, wgrp_ref, pscale_ref, wpp_ref, wpa_ref, wout_ref,
     ngain_ref, nshift_ref, nscale_ref) = refs[15 + n_x:24 + n_x]
    outs_and_scratch = refs[24 + n_x:]
    if final:
        y_ref, band_ref, ext_ref, a0_ref, b0_ref, a1_ref, b1_ref, m_ref, oi_ref, li_ref, wt_ref, sq_ref = outs_and_scratch
    else:
        out_ref, h_ref, band_ref, ext_ref, a0_ref, b0_ref, a1_ref, b1_ref, m_ref, oi_ref, li_ref, wt_ref, sq_ref = outs_and_scratch
    tm = gpool_ref.shape[0]
    hl = POOL_HALO
    gw = GROUP_WIDTH
    hpg = HEADS_PER_GROUP
    s = pl.program_id(0)

    @pl.when(s == 0)
    def _():
        i = lax.broadcasted_iota(jnp.int32, (tm, tm + 2 * hl), 0)
        e = lax.broadcasted_iota(jnp.int32, (tm, tm + 2 * hl), 1)
        off = e - hl - i
        for g, w in enumerate(POOL_WINDOWS):
            band_ref[g] = jnp.where((off >= -(w // 2)) & (off < w // 2), 1.0, 0.0).astype(BF16)
        a0_ref[...] = jnp.zeros_like(a0_ref)
        b0_ref[...] = jnp.zeros_like(b0_ref)

    def step(a_cur, b_cur, a_nxt, b_nxt):
        back_tile = jnp.maximum(s - 1, 0) + block_off
        front_tile = jnp.minimum(s, n_tiles - 1) + block_off
        _, t0, length = _locate(front_tile * tm, lay)
        n_pieces = D_MODEL // MERGE_PIECE

        def piece(c):
            return slice(c * MERGE_PIECE, (c + 1) * MERGE_PIECE)

        def head(hh):
            return slice(hh * HEAD_DIM, (hh + 1) * HEAD_DIM)

        def front_prepare():
            zeros = jnp.zeros((hl, POOL_WIDTH), BF16)
            ext_ref[0:hl] = jnp.where(t0 == 0, zeros, pprev_ref[...])
            ext_ref[hl:hl + tm] = pin_ref[...]
            ext_ref[hl + tm:] = jnp.where(t0 + tm == length, zeros, pnext_ref[...])
            for slot, (o_ref, l_ref, dil) in enumerate(((o1_ref, l1_ref, DILATIONS[1]), (o2_ref, l2_ref, DILATIONS[2]))):
                for r in range(dil):
                    rows = pl.ds(r, tm // dil, stride=dil)
                    for hh in range(hpg):
                        oi_ref[slot * hpg + hh, rows, :] = o_ref[r, :, head(hh)].astype(F32)
                    li_ref[slot, rows, :] = l_ref[r]
            lses = (l0_ref[...], li_ref[0], li_ref[1])
            mx = jnp.maximum(jnp.maximum(lses[0], lses[1]), lses[2])
            es = [jnp.exp(l - mx) for l in lses]
            inv_den = 1.0 / (es[0] + es[1] + es[2])
            for g in range(len(DILATIONS)):
                wt_ref[g] = es[g] * inv_den

        def front_head(g, hh):
            def run():
                ag_ref = (ag0_ref, ag1_ref, ag2_ref)[g]
                o = o0_ref[:, head(hh)].astype(F32) if g == 0 else oi_ref[(g - 1) * hpg + hh]
                gate_half = ag_ref[:, head(hh)].astype(F32)
                b = o * wt_ref[g, :, hh:hh + 1] * (gate_half * _one_plus_tanh(gate_half))
                b_nxt[:, g * gw + hh * HEAD_DIM:g * gw + (hh + 1) * HEAD_DIM] = b.astype(BF16)
            return run

        def front_pool(g):
            def run():
                w = POOL_WINDOWS[g]
                cs = slice(g * POOL_GROUP, (g + 1) * POOL_GROUP)
                t = t0 + lax.broadcasted_iota(jnp.int32, (tm, 1), 0)
                acc = jnp.dot(band_ref[g], ext_ref[:, cs], preferred_element_type=F32)
                cnt = (jnp.minimum(t + w // 2, length) - jnp.maximum(t - w // 2, 0)).astype(F32)
                p = acc * (1.0 / cnt) - pin_ref[:, cs].astype(F32)
                y = jnp.dot(p.astype(BF16), wgrp_ref[g], preferred_element_type=F32)
                gate_half = pgate_ref[:, cs].astype(F32)
                a = y * pscale_ref[:, cs] * (gate_half * _one_plus_tanh(gate_half))
                a_nxt[:, cs] = a.astype(BF16)
            return run

        front = [front_head(g, hh) for g in range(len(DILATIONS)) for hh in range(hpg)]
        front += [front_pool(g) for g in range(len(POOL_WINDOWS))]

        x_dst = y_ref if final else out_ref

        def back_merge(c):
            def run():
                pp = jnp.dot(a_cur[...], wpp_ref[:, piece(c)], preferred_element_type=F32)
                pa = jnp.dot(b_cur[...], wpa_ref[:, piece(c)], preferred_element_type=F32)
                merged = (_one_plus_tanh(gpool_ref[:, piece(c)].astype(F32)) * pp
                          + _one_plus_tanh(gattn_ref[:, piece(c)].astype(F32)) * pa)
                m_ref[:, piece(c)] = merged.astype(BF16)
            return run

        def back_out(c):
            def run():
                out = jnp.dot(m_ref[...], wout_ref[:, piece(c)], preferred_element_type=F32)
                if split_x:
                    x = jnp.where(back_tile * tm < lay[0] * lay[1], x_refs[0][:, piece(c)], x_refs[1][:, piece(c)])
                else:
                    x = x_refs[0][:, piece(c)]
                x_new = x + gate_ref[0, :, piece(c)] * out
                x_dst[:, piece(c)] = x_new
                sq_ref[:, c:c + 1] = jnp.sum(x_new * x_new, axis=-1, keepdims=True)
            return run

        back = [back_merge(c) for c in range(n_pieces)] + [back_out(c) for c in range(n_pieces)]

        front_prepare()
        for k in range(max(len(back), len(front))):
            if k < len(back):
                back[k]()
            if k < len(front):
                front[k]()

        sq = jnp.sum(sq_ref[:, 0:n_pieces], axis=-1, keepdims=True)
        inv_rms = lax.rsqrt(sq * (1.0 / D_MODEL) + NORM_EPS)
        for c in range(n_pieces):
            xn = x_dst[:, piece(c)] * inv_rms * ngain_ref[:, piece(c)]
            if final:
                y_ref[:, piece(c)] = xn
            else:
                h_ref[:, piece(c)] = (xn * (1.0 + nscale_ref[0, :, piece(c)]) + nshift_ref[0, :, piece(c)]).astype(BF16)

    @pl.when(s % 2 == 0)
    def _():
        step(a0_ref, b0_ref, a1_ref, b1_ref)

    @pl.when(s % 2 == 1)
    def _():
        step(a1_ref, b1_ref, a0_ref, b0_ref)


def _merge(z, attn, x, mod_rows, layer, w_grp, pool_scale, w_pp, w_pa, w_out, next_gain, gain_index, lay, *,
           final=False, row_start=0, rows=None):
    split_x = isinstance(x, (tuple, list))
    d = D_MODEL
    t = z.shape[0]
    rows = t if rows is None else rows
    tm = MERGE_TM
    hb = POOL_HALO
    assert row_start % tm == 0 and rows % tm == 0
    off = row_start // tm
    n_tiles = rows // tm
    halo_last = t // hb - 1
    (o0, l0), (o1, l1), (o2, l2) = attn
    d1, d2 = DILATIONS[1], DILATIONS[2]

    def front(s):
        return jnp.minimum(s, n_tiles - 1) + off

    def back(s):
        return jnp.maximum(s - 1, 0) + off

    def zcols(tile, width, col):
        assert col % width == 0
        return pl.BlockSpec((tm, width), lambda s: (tile(s), col // width))

    def const(stacked, index):
        tail = stacked.shape[1:]
        return pl.BlockSpec((None, *tail), lambda s: (index,) + (0,) * len(tail), pipeline_mode=pl.Buffered(1))

    def row(tile, width):
        return pl.BlockSpec((tm, width), lambda s: (tile(s), 0))

    def cls(dil, width):
        return pl.BlockSpec((dil, tm // dil, width), lambda s: (0, front(s), 0))

    next_layer = layer if final else layer + 1
    out_row = pl.BlockSpec((tm, d), lambda s: (back(s) - off, 0))
    if final:
        out_specs = out_row
        out_shape = jax.ShapeDtypeStruct((rows, d), F32)
    else:
        out_specs = [out_row, out_row]
        out_shape = [jax.ShapeDtypeStruct((rows, d), F32), jax.ShapeDtypeStruct((rows, d), BF16)]
    x_specs = _stream_specs(tm, lay, back) if split_x else [row(back, d)]
    x_args = tuple(x) if split_x else (x,)

    return pl.pallas_call(
        functools.partial(_merge_kernel, lay=lay, block_off=off, n_tiles=n_tiles, final=final, split_x=split_x),
        grid=(n_tiles + 1,),
        in_specs=[
            zcols(front, POOL_WIDTH, ZC_POOL_IN),
            pl.BlockSpec((hb, POOL_WIDTH),
                         lambda s: (jnp.maximum(front(s) * (tm // hb) - 1, 0), ZC_POOL_IN // POOL_WIDTH)),
            pl.BlockSpec((hb, POOL_WIDTH),
                         lambda s: (jnp.minimum((front(s) + 1) * (tm // hb), halo_last), ZC_POOL_IN // POOL_WIDTH)),
            zcols(front, POOL_WIDTH, ZC_POOL_GATE),
            zcols(front, GROUP_WIDTH, ZC_AG0),
            zcols(front, GROUP_WIDTH, ZC_AG1),
            zcols(front, GROUP_WIDTH, ZC_AG2),
            row(front, GROUP_WIDTH), cls(d1, GROUP_WIDTH), cls(d2, GROUP_WIDTH),
            row(front, HEAD_DIM), cls(d1, HEAD_DIM), cls(d2, HEAD_DIM),
            zcols(back, D_MODEL, ZC_G_POOL),
            zcols(back, D_MODEL, ZC_G_ATTN),
            *x_specs,
            _mod_spec(layer, 2, tm, lay, back),
            const(w_grp, layer), const(pool_scale, layer), const(w_pp, layer), const(w_pa, layer),
            const(w_out, layer), const(next_gain, gain_index),
            _mod_spec(next_layer, 0, tm, lay, back),
            _mod_spec(next_layer, 1, tm, lay, back),
        ],
        out_specs=out_specs,
        out_shape=out_shape,
        scratch_shapes=[
            pltpu.VMEM((len(POOL_WINDOWS), tm, tm + 2 * POOL_HALO), BF16),
            pltpu.VMEM((tm + 2 * POOL_HALO, POOL_WIDTH), BF16),
            pltpu.VMEM((tm, POOL_WIDTH), BF16),
            pltpu.VMEM((tm, ATTN_WIDTH), BF16),
            pltpu.VMEM((tm, POOL_WIDTH), BF16),
            pltpu.VMEM((tm, ATTN_WIDTH), BF16),
            pltpu.VMEM((tm, D_MODEL), BF16),
            pltpu.VMEM((2 * HEADS_PER_GROUP, tm, HEAD_DIM), F32),
            pltpu.VMEM((2, tm, HEAD_DIM), F32),
            pltpu.VMEM((len(DILATIONS), tm, HEAD_DIM), F32),
            pltpu.VMEM((tm, HEAD_DIM), F32),
        ],
        compiler_params=pltpu.CompilerParams(
            dimension_semantics=("arbitrary",), vmem_limit_bytes=VMEM_LIMIT_BYTES),
        name="merge_final" if final else "merge",
    )(z, z, z, z, z, z, z, o0, o1, o2, l0, l1, l2, z, z, *x_args, mod_rows,
      w_grp, pool_scale, w_pp, w_pa, w_out, next_gain, mod_rows, mod_rows)


def kernel(x_prompt, x_sample, c_prompt, c_sample, norm_gain, w_ada, b_ada, w_in, w_pool_grp, pool_scale,
           w_proj_pool, w_proj_attn, w_out, final_gain):
    bp, sp, d = x_prompt.shape
    bs, ss, _ = x_sample.shape
    assert d == D_MODEL
    lay = (bp, sp, bs, ss)
    depth = w_in.shape[0]
    nseq = bp + bs
    gw = GROUP_WIDTH
    for s in (sp, ss):
        assert s % INPROJ_TM == 0 and s % MERGE_TM == 0 and s % (max(DILATIONS) * ATTN_SUB) == 0

    x = (x_prompt.reshape(bp * sp, d), x_sample.reshape(bs * ss, d))
    c = jnp.concatenate([c_prompt, c_sample], axis=0)
    mod_rows = _modulation(c, w_ada, b_ada).reshape(depth * nseq * 3, 1, d)
    cos, sin = _rope_tables(max(sp, ss))

    weights = (w_pool_grp.astype(BF16), pool_scale.reshape(depth, 1, POOL_WIDTH), (0.5 * w_proj_pool).astype(BF16),
               (0.5 * w_proj_attn).astype(BF16), w_out.astype(BF16))
    gains = norm_gain.reshape(depth, 1, d)

    h = _prenorm(x, norm_gain[0], mod_rows, lay)
    for layer in range(depth):
        z, qk1, qk2, v1, v2 = _in_proj(h, w_in, layer, cos, sin, lay)
        o0, l0 = _band_attention(z[None], z[None], ZC_Q0 // gw, ZC_K0 // gw, ZC_V0 // gw, DILATIONS[0], lay)
        attn = [
            (o0[0], l0[0]),
            _band_attention(qk1, v1, 0, 1, 0, DILATIONS[1], lay),
            _band_attention(qk2, v2, 0, 1, 0, DILATIONS[2], lay),
        ]
        if layer + 1 < depth:
            x, h = _merge(z, attn, x, mod_rows, layer, *weights, gains, layer + 1, lay)
        else:
            last = (z, attn, x, mod_rows, layer, *weights, final_gain.reshape(1, 1, d), 0, lay)
            y_prompt = _merge(*last, final=True, row_start=0, rows=bp * sp)
            y_sample = _merge(*last, final=True, row_start=bp * sp, rows=bs * ss)
    return (y_prompt.reshape(bp, sp, d), y_sample.reshape(bs, ss, d))
```

```python
import functools
import math

import jax
import jax.numpy as jnp
from jax import lax
from jax.experimental import pallas as pl
from jax.experimental.pallas import tpu as pltpu

F32 = jnp.float32
BF16 = jnp.bfloat16

D_MODEL = 2048
POOL_WIDTH = D_MODEL // 2
POOL_WINDOWS = (2, 4, 8, 16)
POOL_GROUP = POOL_WIDTH // len(POOL_WINDOWS)
POOL_HALO = 64
assert POOL_HALO >= max(POOL_WINDOWS) // 2
HEAD_DIM = 128
ATTN_GROUPS = ((128, 1), (512, 4), (2048, 16))
DILATIONS = tuple(dil for _, dil in ATTN_GROUPS)
HEADS_PER_GROUP = 4
GROUP_WIDTH = HEADS_PER_GROUP * HEAD_DIM
ATTN_WIDTH = GROUP_WIDTH * len(ATTN_GROUPS)
BAND_RADIUS = 64
assert all(win // (2 * dil) == BAND_RADIUS for win, dil in ATTN_GROUPS)
assert DILATIONS[0] == 1 and len(DILATIONS) == 3
ROPE_THETA = 10000.0
NORM_EPS = 1e-6
NEG_BIG = -1e30
IN_WIDTH = 2 * POOL_WIDTH + 4 * ATTN_WIDTH + 2 * D_MODEL

_CH_POOL_IN, _CH_POOL_GATE, _CH_Q, _CH_K, _CH_V, _CH_AG, _CH_GP, _CH_GA = 0, 2, 4, 7, 10, 13, 16, 20
IN_PROJ_CHUNK_ORDER = (
    _CH_Q, _CH_K, _CH_Q + 1, _CH_K + 1, _CH_Q + 2, _CH_K + 2, _CH_V + 1, _CH_V + 2,
    _CH_V, _CH_AG, *range(_CH_GP, _CH_GP + 4), *range(_CH_GA, _CH_GA + 4),
    _CH_POOL_IN, _CH_POOL_IN + 1, _CH_POOL_GATE, _CH_POOL_GATE + 1, _CH_AG + 1, _CH_AG + 2)
assert sorted(IN_PROJ_CHUNK_ORDER) == list(range(IN_WIDTH // GROUP_WIDTH))
HALVED_CHUNKS = frozenset((*range(_CH_POOL_GATE, _CH_POOL_GATE + 2), *range(_CH_AG, _CH_AG + 3),
                           *range(_CH_GP, _CH_GP + 4), *range(_CH_GA, _CH_GA + 4)))
IN_PROJ_CHUNK_SCALE = tuple(0.5 if c in HALVED_CHUNKS else 1.0 for c in IN_PROJ_CHUNK_ORDER)
N_SPECIAL_TILES = 4
Z_WIDTH = IN_WIDTH - 6 * GROUP_WIDTH
ZC_Q0, ZC_K0, ZC_V0, ZC_AG0 = 0, GROUP_WIDTH, 2 * GROUP_WIDTH, 3 * GROUP_WIDTH
ZC_G_POOL = 4 * GROUP_WIDTH
ZC_G_ATTN = ZC_G_POOL + D_MODEL
ZC_POOL_IN = ZC_G_ATTN + D_MODEL
ZC_POOL_GATE = ZC_POOL_IN + POOL_WIDTH
ZC_AG1 = ZC_POOL_GATE + POOL_WIDTH
ZC_AG2 = ZC_AG1 + GROUP_WIDTH

VMEM_LIMIT_BYTES = 60 * 1024 * 1024

INPROJ_TM = 1024
INPROJ_TN = 2 * GROUP_WIDTH
INPROJ_PIECE = 256
ATTN_BQ = 1536
ATTN_SUB = 128
MERGE_TM = 256
MERGE_PIECE = 256
NORM_TM = 512


def _one_plus_tanh(x_half):
    return 1.0 + jnp.tanh(x_half)


def _mod_norm(x, gain, scale, shift):
    xn = x * lax.rsqrt(jnp.mean(x * x, axis=-1, keepdims=True) + NORM_EPS) * gain
    return xn * (1.0 + scale) + shift


def _locate(row0, lay, dil=1):
    bp, sp, bs, ss = lay
    rows_p, lp, ls = bp * sp // dil, sp // dil, ss // dil
    in_p = row0 < rows_p
    seq = jnp.where(in_p, row0 // lp, bp + (row0 - rows_p) // ls)
    local = jnp.where(in_p, row0 % lp, (row0 - rows_p) % ls)
    length = jnp.where(in_p, lp, ls)
    return seq, local, length


def _select_by_tile(j, table):
    out = table[0]
    for k in range(1, len(table)):
        out = jnp.where(j == k, table[k], out)
    return out


def _mod_spec(layer, which, tm, lay, tile=lambda i: i):
    nseq = lay[0] + lay[2]

    def idx(i):
        seq, _, _ = _locate(tile(i) * tm, lay)
        return ((layer * nseq + seq) * 3 + which, 0, 0)

    return pl.BlockSpec((1, 1, D_MODEL), idx)


def _stream_specs(tm, lay, tile=lambda i: i):
    tiles_p = lay[0] * lay[1] // tm
    tiles_s = lay[2] * lay[3] // tm
    return [
        pl.BlockSpec((tm, D_MODEL), lambda i: (jnp.minimum(tile(i), tiles_p - 1), 0)),
        pl.BlockSpec((tm, D_MODEL), lambda i: (jnp.clip(tile(i) - tiles_p, 0, tiles_s - 1), 0)),
    ]


def _stream_tile(xp_ref, xs_ref, tile, tm, lay):
    return jnp.where(tile * tm < lay[0] * lay[1], xp_ref[...], xs_ref[...])


def _mod_kernel(c_ref, w_ref, b_ref, o_ref):
    c = c_ref[...].astype(BF16)
    w = w_ref[0].astype(BF16)
    o_ref[0] = jnp.dot(c, w, preferred_element_type=F32) + b_ref[0]


def _modulation(c, w_ada, b_ada):
    depth, d, n = w_ada.shape
    nseq = c.shape[0]
    tn = 1536
    return pl.pallas_call(
        _mod_kernel,
        grid=(depth, n // tn),
        in_specs=[
            pl.BlockSpec((nseq, d), lambda l, j: (0, 0)),
            pl.BlockSpec((1, d, tn), lambda l, j: (l, 0, j)),
            pl.BlockSpec((1, 1, tn), lambda l, j: (l, 0, j)),
        ],
        out_specs=pl.BlockSpec((1, nseq, tn), lambda l, j: (l, 0, j)),
        out_shape=jax.ShapeDtypeStruct((depth, nseq, n), F32),
        compiler_params=pltpu.CompilerParams(
            dimension_semantics=("arbitrary", "arbitrary"), vmem_limit_bytes=VMEM_LIMIT_BYTES),
        name="modulation",
    )(c, w_ada, b_ada.reshape(depth, 1, n))


def _rope_table_kernel(cos_ref, sin_ref):
    s = cos_ref.shape[0]
    half = HEAD_DIM // 2
    pos = lax.broadcasted_iota(jnp.int32, (s, HEAD_DIM), 0).astype(F32)
    lane = lax.broadcasted_iota(jnp.int32, (s, HEAD_DIM), 1)
    idx = jnp.where(lane < half, lane, lane - half).astype(F32)
    inv = jnp.exp(idx * (-math.log(ROPE_THETA) / half))
    ang = pos * inv
    cos_ref[...] = jnp.cos(ang)
    sin_ref[...] = jnp.where(lane < half, -jnp.sin(ang), jnp.sin(ang))


def _rope_tables(s):
    return pl.pallas_call(
        _rope_table_kernel,
        out_shape=(jax.ShapeDtypeStruct((s, HEAD_DIM), F32), jax.ShapeDtypeStruct((s, HEAD_DIM), F32)),
        name="rope_tables",
    )()


def _prenorm_kernel(xp_ref, xs_ref, g_ref, shift_ref, scale_ref, h_ref, *, lay):
    tm = h_ref.shape[0]
    x = _stream_tile(xp_ref, xs_ref, pl.program_id(0), tm, lay)
    h_ref[...] = _mod_norm(x, g_ref[...], scale_ref[0], shift_ref[0]).astype(BF16)


def _prenorm(x_streams, gain, mod_rows, lay):
    d = D_MODEL
    t = lay[0] * lay[1] + lay[2] * lay[3]
    tm = NORM_TM
    return pl.pallas_call(
        functools.partial(_prenorm_kernel, lay=lay),
        grid=(t // tm,),
        in_specs=[
            *_stream_specs(tm, lay),
            pl.BlockSpec((1, d), lambda i: (0, 0)),
            _mod_spec(0, 0, tm, lay),
            _mod_spec(0, 1, tm, lay),
        ],
        out_specs=pl.BlockSpec((tm, d), lambda i: (i, 0)),
        out_shape=jax.ShapeDtypeStruct((t, d), BF16),
        compiler_params=pltpu.CompilerParams(
            dimension_semantics=("parallel",), vmem_limit_bytes=VMEM_LIMIT_BYTES),
        name="prenorm",
    )(*x_streams, gain.reshape(1, d), mod_rows, mod_rows)


def _inproj_kernel(chunk_tab_ref, h_ref, wa_ref, wb_ref, cos_ref, sin_ref, z_ref, qk1_ref, qk2_ref, v1_ref, v2_ref,
                   w_ref, s_ref, t_ref, *, lay):
    del chunk_tab_ref
    j = pl.program_id(0)
    n_chunks, tm, _ = s_ref.shape
    d1, d2 = DILATIONS[1], DILATIONS[2]
    hpg = HEADS_PER_GROUP
    per_piece = INPROJ_PIECE // HEAD_DIM

    @pl.when(pl.program_id(1) == 0)
    def _():
        for half, src_ref in enumerate((wa_ref, wb_ref)):
            w_ref[half] = (src_ref[...] * _select_by_tile(j, IN_PROJ_CHUNK_SCALE[half::2])).astype(BF16)

    def chunk(hh):
        return slice(hh * HEAD_DIM, (hh + 1) * HEAD_DIM)

    def pieces():
        pieces_per_half = GROUP_WIDTH // INPROJ_PIECE
        for c in range(n_chunks // per_piece):
            c_in = c % pieces_per_half
            cols = slice(c_in * INPROJ_PIECE, (c_in + 1) * INPROJ_PIECE)
            acc = jnp.dot(h_ref[...], w_ref[c // pieces_per_half, :, cols], preferred_element_type=F32)
            for k in range(per_piece):
                yield c * per_piece + k, acc[:, chunk(k)]

    def rope(xh):
        _, local, _ = _locate(pl.program_id(1) * tm, lay)
        rows = pl.ds(pl.multiple_of(local, tm), tm)
        return xh * cos_ref[rows, :] + pltpu.roll(xh, HEAD_DIM // 2, 1) * sin_ref[rows, :]

    def to_classes(dst_ref, dil, hh, c_out):
        if dil == d1 * d1:
            for r1 in range(d1):
                t_ref[hh % 2, r1] = s_ref[hh, pl.ds(r1, tm // d1, stride=d1), :]
            for r in range(dil):
                rows = pl.ds(r // d1, tm // dil, stride=d1)
                dst_ref[r, :, chunk(c_out)] = t_ref[hh % 2, r % d1, rows, :].astype(BF16)
        else:
            for r in range(dil):
                dst_ref[r, :, chunk(c_out)] = s_ref[hh, pl.ds(r, tm // dil, stride=dil), :].astype(BF16)

    @pl.when(j == 0)
    def _():
        for hh, xh in pieces():
            z_ref[:, chunk(hh)] = rope(xh).astype(BF16)

    @pl.when(j == 1)
    def _():
        for hh, xh in pieces():
            s_ref[hh] = rope(xh)
            to_classes(qk1_ref, d1, hh, hh)

    @pl.when(j == 2)
    def _():
        for hh, xh in pieces():
            s_ref[hh] = rope(xh)
            to_classes(qk2_ref, d2, hh, hh)

    @pl.when(j == 3)
    def _():
        for hh, xh in pieces():
            s_ref[hh] = xh
            if hh < hpg:
                to_classes(v1_ref, d1, hh, hh)
            else:
                to_classes(v2_ref, d2, hh, hh - hpg)

    @pl.when(j >= N_SPECIAL_TILES)
    def _():
        for hh, xh in pieces():
            z_ref[:, chunk(hh)] = xh.astype(BF16)


def _in_proj(h, w_in, layer, cos, sin, lay):
    t, d = h.shape
    n = w_in.shape[2]
    tm, tn, gw = INPROJ_TM, INPROJ_TN, GROUP_WIDTH
    d1, d2 = DILATIONS[1], DILATIONS[2]
    last = t // tm - 1

    def z_idx(j, i, tab):
        special = (j >= 1) & (j < N_SPECIAL_TILES)
        return (jnp.where(special, last, i), jnp.maximum(j - (N_SPECIAL_TILES - 1), 0))

    def cls_spec(dil, width, written_at):
        return pl.BlockSpec(
            (dil, tm // dil, width),
            lambda j, i, tab: (0, jnp.where(j < written_at, 0, jnp.where(j == written_at, i, last)), 0))

    def cls_shape(dil, width):
        return jax.ShapeDtypeStruct((dil, t // dil, width), BF16)

    rope_table_spec = pl.BlockSpec(cos.shape, lambda j, i, tab: (0, 0), pipeline_mode=pl.Buffered(1))

    def w_half_spec(half):
        return pl.BlockSpec((None, d, gw), lambda j, i, tab: (layer, 0, tab[2 * j + half]))

    grid_spec = pltpu.PrefetchScalarGridSpec(
        num_scalar_prefetch=1,
        grid=(n // tn, t // tm),
        in_specs=[
            pl.BlockSpec((tm, d), lambda j, i, tab: (i, 0)),
            w_half_spec(0),
            w_half_spec(1),
            rope_table_spec,
            rope_table_spec,
        ],
        out_specs=[
            pl.BlockSpec((tm, tn), z_idx),
            cls_spec(d1, 2 * gw, 1), cls_spec(d2, 2 * gw, 2), cls_spec(d1, gw, 3), cls_spec(d2, gw, 3),
        ],
        scratch_shapes=[pltpu.VMEM((2, d, gw), BF16),
                        pltpu.VMEM((tn // HEAD_DIM, tm, HEAD_DIM), F32),
                        pltpu.VMEM((2, d1, tm // d1, HEAD_DIM), F32)],
    )
    return pl.pallas_call(
        functools.partial(_inproj_kernel, lay=lay),
        grid_spec=grid_spec,
        out_shape=[
            jax.ShapeDtypeStruct((t, Z_WIDTH), BF16),
            cls_shape(d1, 2 * gw), cls_shape(d2, 2 * gw), cls_shape(d1, gw), cls_shape(d2, gw),
        ],
        compiler_params=pltpu.CompilerParams(
            dimension_semantics=("arbitrary", "arbitrary"), vmem_limit_bytes=VMEM_LIMIT_BYTES),
        name="in_proj",
    )(jnp.asarray(IN_PROJ_CHUNK_ORDER, jnp.int32), h, w_in, w_in, cos, sin)


def _attn_kernel(q_ref, kp_ref, km_ref, kn_ref, vp_ref, vm_ref, vn_ref, o_ref, lse_ref, k_buf, v_buf, *, dil, lay):
    bq = q_ref.shape[0]
    r = BAND_RADIUS
    k_buf[0:r] = kp_ref[...]
    k_buf[r:r + bq] = km_ref[...]
    k_buf[r + bq:] = kn_ref[...]
    v_buf[0:r] = vp_ref[...]
    v_buf[r:r + bq] = vm_ref[...]
    v_buf[r + bq:] = vn_ref[...]

    nk = ATTN_SUB + 2 * r
    qi = lax.broadcasted_iota(jnp.int32, (ATTN_SUB, nk), 0)
    kj = lax.broadcasted_iota(jnp.int32, (ATTN_SUB, nk), 1) - r
    in_band = jnp.abs(kj - qi) <= r
    inv_sqrt = 1.0 / math.sqrt(HEAD_DIM)

    for sb in range(bq // ATTN_SUB):
        a = sb * ATTN_SUB
        _, local, length = _locate(pl.program_id(1) * bq + a, lay, dil)
        mask = in_band & (kj + local >= 0) & (kj + local < length)
        lse_ref[a:a + ATTN_SUB, :] = jnp.zeros((ATTN_SUB, HEAD_DIM), F32)
        for hh in range(HEADS_PER_GROUP):
            cs = slice(hh * HEAD_DIM, (hh + 1) * HEAD_DIM)
            q = q_ref[a:a + ATTN_SUB, cs]
            k = k_buf[a:a + nk, cs]
            v = v_buf[a:a + nk, cs]
            s = lax.dot_general(q, k, (((1,), (1,)), ((), ())), preferred_element_type=F32) * inv_sqrt
            s = jnp.where(mask, s, NEG_BIG)
            m = jnp.max(s, axis=-1, keepdims=True)
            p = jnp.exp(s - m)
            l = jnp.sum(p, axis=-1, keepdims=True)
            o = jnp.dot(p.astype(BF16), v, preferred_element_type=F32) * (1.0 / l)
            o_ref[a:a + ATTN_SUB, cs] = o.astype(BF16)
            lse_ref[a:a + ATTN_SUB, hh:hh + 1] = m + jnp.log(l)


def _band_attention(qk, v, q_col, k_col, v_col, dil, lay):
    rows = qk.shape[1]
    bq, r, gw = ATTN_BQ, BAND_RADIUS, GROUP_WIDTH
    assert rows % bq == 0 and bq % r == 0
    halo_last = rows // r - 1

    def main(col):
        return pl.BlockSpec((None, bq, gw), lambda c, n: (c, n, col))

    def prev(col):
        return pl.BlockSpec((None, r, gw), lambda c, n: (c, jnp.maximum(n * (bq // r) - 1, 0), col))

    def nxt(col):
        return pl.BlockSpec((None, r, gw), lambda c, n: (c, jnp.minimum((n + 1) * (bq // r), halo_last), col))

    return pl.pallas_call(
        functools.partial(_attn_kernel, dil=dil, lay=lay),
        grid=(dil, rows // bq),
        in_specs=[main(q_col), prev(k_col), main(k_col), nxt(k_col), prev(v_col), main(v_col), nxt(v_col)],
        out_specs=[
            pl.BlockSpec((None, bq, gw), lambda c, n: (c, n, 0)),
            pl.BlockSpec((None, bq, HEAD_DIM), lambda c, n: (c, n, 0)),
        ],
        out_shape=[
            jax.ShapeDtypeStruct((dil, rows, gw), BF16),
            jax.ShapeDtypeStruct((dil, rows, HEAD_DIM), F32),
        ],
        scratch_shapes=[pltpu.VMEM((bq + 2 * r, gw), BF16), pltpu.VMEM((bq + 2 * r, gw), BF16)],
        compiler_params=pltpu.CompilerParams(
            dimension_semantics=("parallel", "parallel"), vmem_limit_bytes=VMEM_LIMIT_BYTES),
        name=f"band_attn_d{dil}",
    )(qk, qk, qk, qk, v, v, v)


def _merge_kernel(*refs, lay, block_off, n_tiles, final, split_x, layer, next_layer):
    n_x = 2 if split_x else 1
    (pin_ref, pprev_ref, pnext_ref, pgate_ref, ag0_ref, ag1_ref, ag2_ref,
     o0_ref, o1_ref, o2_ref, l0_ref, l1_ref, l2_ref, gpool_ref, gattn_ref) = refs[:15]
    x_refs = refs[15:15 + n_x]
    (mod_ref, wgrp_ref, pscale_ref, wpp_ref, wpa_ref, wout_ref, ngain_ref) = refs[15 + n_x:22 + n_x]
    outs_and_scratch = refs[22 + n_x:]
    if final:
        y_ref, band_ref, ext_ref, a0_ref, b0_ref, a1_ref, b1_ref, m_ref, oi_ref, li_ref, wt_ref, sq_ref = outs_and_scratch
    else:
        out_ref, h_ref, band_ref, ext_ref, a0_ref, b0_ref, a1_ref, b1_ref, m_ref, oi_ref, li_ref, wt_ref, sq_ref = outs_and_scratch
    tm = gpool_ref.shape[0]
    hl = POOL_HALO
    gw = GROUP_WIDTH
    hpg = HEADS_PER_GROUP
    s = pl.program_id(0)

    @pl.when(s == 0)
    def _():
        i = lax.broadcasted_iota(jnp.int32, (tm, tm + 2 * hl), 0)
        e = lax.broadcasted_iota(jnp.int32, (tm, tm + 2 * hl), 1)
        off = e - hl - i
        for g, w in enumerate(POOL_WINDOWS):
            band_ref[g] = jnp.where((off >= -(w // 2)) & (off < w // 2), 1.0, 0.0).astype(BF16)
        a0_ref[...] = jnp.zeros_like(a0_ref)
        b0_ref[...] = jnp.zeros_like(b0_ref)

    def step(a_cur, b_cur, a_nxt, b_nxt):
        back_tile = jnp.maximum(s - 1, 0) + block_off
        back_seq, _, _ = _locate(back_tile * tm, lay)
        nseq = lay[0] + lay[2]
        gate_row = pl.ds((layer * nseq + back_seq) * 3 + 2, 1)
        nshift_row = pl.ds((next_layer * nseq + back_seq) * 3, 1)
        nscale_row = pl.ds((next_layer * nseq + back_seq) * 3 + 1, 1)
        front_tile = jnp.minimum(s, n_tiles - 1) + block_off
        _, t0, length = _locate(front_tile * tm, lay)
        n_pieces = D_MODEL // MERGE_PIECE

        def piece(c):
            return slice(c * MERGE_PIECE, (c + 1) * MERGE_PIECE)

        def head(hh):
            return slice(hh * HEAD_DIM, (hh + 1) * HEAD_DIM)

        def front_prepare():
            zeros = jnp.zeros((hl, POOL_WIDTH), BF16)
            ext_ref[0:hl] = jnp.where(t0 == 0, zeros, pprev_ref[...])
            ext_ref[hl:hl + tm] = pin_ref[...]
            ext_ref[hl + tm:] = jnp.where(t0 + tm == length, zeros, pnext_ref[...])
            for slot, (o_ref, l_ref, dil) in enumerate(((o1_ref, l1_ref, DILATIONS[1]), (o2_ref, l2_ref, DILATIONS[2]))):
                for r in range(dil):
                    rows = pl.ds(r, tm // dil, stride=dil)
                    for hh in range(hpg):
                        oi_ref[slot * hpg + hh, rows, :] = o_ref[r, :, head(hh)].astype(F32)
                    li_ref[slot, rows, :] = l_ref[r]
            lses = (l0_ref[...], li_ref[0], li_ref[1])
            mx = jnp.maximum(jnp.maximum(lses[0], lses[1]), lses[2])
            es = [jnp.exp(l - mx) for l in lses]
            inv_den = 1.0 / (es[0] + es[1] + es[2])
            for g in range(len(DILATIONS)):
                wt_ref[g] = es[g] * inv_den

        def front_head(g, hh):
            def run():
                ag_ref = (ag0_ref, ag1_ref, ag2_ref)[g]
                o = o0_ref[:, head(hh)].astype(F32) if g == 0 else oi_ref[(g - 1) * hpg + hh]
                gate_half = ag_ref[:, head(hh)].astype(F32)
                b = o * wt_ref[g, :, hh:hh + 1] * (gate_half * _one_plus_tanh(gate_half))
                b_nxt[:, g * gw + hh * HEAD_DIM:g * gw + (hh + 1) * HEAD_DIM] = b.astype(BF16)
            return run

        def front_pool(g):
            def run():
                w = POOL_WINDOWS[g]
                cs = slice(g * POOL_GROUP, (g + 1) * POOL_GROUP)
                t = t0 + lax.broadcasted_iota(jnp.int32, (tm, 1), 0)
                acc = jnp.dot(band_ref[g], ext_ref[:, cs], preferred_element_type=F32)
                cnt = (jnp.minimum(t + w // 2, length) - jnp.maximum(t - w // 2, 0)).astype(F32)
                p = acc * (1.0 / cnt) - pin_ref[:, cs].astype(F32)
                y = jnp.dot(p.astype(BF16), wgrp_ref[g], preferred_element_type=F32)
                gate_half = pgate_ref[:, cs].astype(F32)
                a = y * pscale_ref[:, cs] * (gate_half * _one_plus_tanh(gate_half))
                a_nxt[:, cs] = a.astype(BF16)
            return run

        front = [front_head(g, hh) for g in range(len(DILATIONS)) for hh in range(hpg)]
        front += [front_pool(g) for g in range(len(POOL_WINDOWS))]

        x_dst = y_ref if final else out_ref

        def back_merge(c):
            def run():
                pp = jnp.dot(a_cur[...], wpp_ref[:, piece(c)], preferred_element_type=F32)
                pa = jnp.dot(b_cur[...], wpa_ref[:, piece(c)], preferred_element_type=F32)
                merged = (_one_plus_tanh(gpool_ref[:, piece(c)].astype(F32)) * pp
                          + _one_plus_tanh(gattn_ref[:, piece(c)].astype(F32)) * pa)
                m_ref[:, piece(c)] = merged.astype(BF16)
            return run

        def back_out(c):
            def run():
                out = jnp.dot(m_ref[...], wout_ref[:, piece(c)], preferred_element_type=F32)
                if split_x:
                    x = jnp.where(back_tile * tm < lay[0] * lay[1], x_refs[0][:, piece(c)], x_refs[1][:, piece(c)])
                else:
                    x = x_refs[0][:, piece(c)]
                x_new = x + mod_ref[gate_row, piece(c)] * out
                x_dst[:, piece(c)] = x_new
                sq_ref[:, c:c + 1] = jnp.sum(x_new * x_new, axis=-1, keepdims=True)
            return run

        back = [back_merge(c) for c in range(n_pieces)] + [back_out(c) for c in range(n_pieces)]

        front_prepare()
        for k in range(max(len(back), len(front))):
            if k < len(back):
                back[k]()
            if k < len(front):
                front[k]()

        sq = jnp.sum(sq_ref[:, 0:n_pieces], axis=-1, keepdims=True)
        inv_rms = lax.rsqrt(sq * (1.0 / D_MODEL) + NORM_EPS)
        for c in range(n_pieces):
            xn = x_dst[:, piece(c)] * inv_rms * ngain_ref[:, piece(c)]
            if final:
                y_ref[:, piece(c)] = xn
            else:
                h_ref[:, piece(c)] = (xn * (1.0 + mod_ref[nscale_row, piece(c)]) + mod_ref[nshift_row, piece(c)]).astype(BF16)

    @pl.when(s % 2 == 0)
    def _():
        step(a0_ref, b0_ref, a1_ref, b1_ref)

    @pl.when(s % 2 == 1)
    def _():
        step(a1_ref, b1_ref, a0_ref, b0_ref)


def _merge(z, attn, x, mod_rows, layer, w_grp, pool_scale, w_pp, w_pa, w_out, next_gain, gain_index, lay, *,
           final=False, row_start=0, rows=None):
    split_x = isinstance(x, (tuple, list))
    d = D_MODEL
    t = z.shape[0]
    rows = t if rows is None else rows
    tm = MERGE_TM
    hb = POOL_HALO
    assert row_start % tm == 0 and rows % tm == 0
    off = row_start // tm
    n_tiles = rows // tm
    halo_last = t // hb - 1
    (o0, l0), (o1, l1), (o2, l2) = attn
    d1, d2 = DILATIONS[1], DILATIONS[2]

    def front(s):
        return jnp.minimum(s, n_tiles - 1) + off

    def back(s):
        return jnp.maximum(s - 1, 0) + off

    def zcols(tile, width, col):
        assert col % width == 0
        return pl.BlockSpec((tm, width), lambda s: (tile(s), col // width))

    def const(stacked, index):
        tail = stacked.shape[1:]
        return pl.BlockSpec((None, *tail), lambda s: (index,) + (0,) * len(tail), pipeline_mode=pl.Buffered(1))

    def row(tile, width):
        return pl.BlockSpec((tm, width), lambda s: (tile(s), 0))

    def cls(dil, width):
        return pl.BlockSpec((dil, tm // dil, width), lambda s: (0, front(s), 0))

    next_layer = layer if final else layer + 1
    mod_table = mod_rows.reshape(mod_rows.shape[0], d)
    out_row = pl.BlockSpec((tm, d), lambda s: (back(s) - off, 0))
    if final:
        out_specs = out_row
        out_shape = jax.ShapeDtypeStruct((rows, d), F32)
    else:
        out_specs = [out_row, out_row]
        out_shape = [jax.ShapeDtypeStruct((rows, d), F32), jax.ShapeDtypeStruct((rows, d), BF16)]
    x_specs = _stream_specs(tm, lay, back) if split_x else [row(back, d)]
    x_args = tuple(x) if split_x else (x,)

    return pl.pallas_call(
        functools.partial(_merge_kernel, lay=lay, block_off=off, n_tiles=n_tiles, final=final, split_x=split_x,
                          layer=layer, next_layer=next_layer),
        grid=(n_tiles + 1,),
        in_specs=[
            zcols(front, POOL_WIDTH, ZC_POOL_IN),
            pl.BlockSpec((hb, POOL_WIDTH),
                         lambda s: (jnp.maximum(front(s) * (tm // hb) - 1, 0), ZC_POOL_IN // POOL_WIDTH)),
            pl.BlockSpec((hb, POOL_WIDTH),
                         lambda s: (jnp.minimum((front(s) + 1) * (tm // hb), halo_last), ZC_POOL_IN // POOL_WIDTH)),
            zcols(front, POOL_WIDTH, ZC_POOL_GATE),
            zcols(front, GROUP_WIDTH, ZC_AG0),
            zcols(front, GROUP_WIDTH, ZC_AG1),
            zcols(front, GROUP_WIDTH, ZC_AG2),
            row(front, GROUP_WIDTH), cls(d1, GROUP_WIDTH), cls(d2, GROUP_WIDTH),
            row(front, HEAD_DIM), cls(d1, HEAD_DIM), cls(d2, HEAD_DIM),
            zcols(back, D_MODEL, ZC_G_POOL),
            zcols(back, D_MODEL, ZC_G_ATTN),
            *x_specs,
            pl.BlockSpec(mod_table.shape, lambda s: (0, 0), pipeline_mode=pl.Buffered(1)),
            const(w_grp, layer), const(pool_scale, layer), const(w_pp, layer), const(w_pa, layer),
            const(w_out, layer), const(next_gain, gain_index),
        ],
        out_specs=out_specs,
        out_shape=out_shape,
        scratch_shapes=[
            pltpu.VMEM((len(POOL_WINDOWS), tm, tm + 2 * POOL_HALO), BF16),
            pltpu.VMEM((tm + 2 * POOL_HALO, POOL_WIDTH), BF16),
            pltpu.VMEM((tm, POOL_WIDTH), BF16),
            pltpu.VMEM((tm, ATTN_WIDTH), BF16),
            pltpu.VMEM((tm, POOL_WIDTH), BF16),
            pltpu.VMEM((tm, ATTN_WIDTH), BF16),
            pltpu.VMEM((tm, D_MODEL), BF16),
            pltpu.VMEM((2 * HEADS_PER_GROUP, tm, HEAD_DIM), F32),
            pltpu.VMEM((2, tm, HEAD_DIM), F32),
            pltpu.VMEM((len(DILATIONS), tm, HEAD_DIM), F32),
            pltpu.VMEM((tm, HEAD_DIM), F32),
        ],
        compiler_params=pltpu.CompilerParams(
            dimension_semantics=("arbitrary",), vmem_limit_bytes=VMEM_LIMIT_BYTES),
        name="merge_final" if final else "merge",
    )(z, z, z, z, z, z, z, o0, o1, o2, l0, l1, l2, z, z, *x_args, mod_table,
      w_grp, pool_scale, w_pp, w_pa, w_out, next_gain)


def kernel(x_prompt, x_sample, c_prompt, c_sample, norm_gain, w_ada, b_ada, w_in, w_pool_grp, pool_scale,
           w_proj_pool, w_proj_attn, w_out, final_gain):
    bp, sp, d = x_prompt.shape
    bs, ss, _ = x_sample.shape
    assert d == D_MODEL
    lay = (bp, sp, bs, ss)
    depth = w_in.shape[0]
    nseq = bp + bs
    gw = GROUP_WIDTH
    for s in (sp, ss):
        assert s % INPROJ_TM == 0 and s % MERGE_TM == 0 and s % (max(DILATIONS) * ATTN_SUB) == 0

    x = (x_prompt.reshape(bp * sp, d), x_sample.reshape(bs * ss, d))
    c = jnp.concatenate([c_prompt, c_sample], axis=0)
    mod_rows = _modulation(c, w_ada, b_ada).reshape(depth * nseq * 3, 1, d)
    cos, sin = _rope_tables(max(sp, ss))

    weights = (w_pool_grp.astype(BF16), pool_scale.reshape(depth, 1, POOL_WIDTH), (0.5 * w_proj_pool).astype(BF16),
               (0.5 * w_proj_attn).astype(BF16), w_out.astype(BF16))
    gains = norm_gain.reshape(depth, 1, d)

    h = _prenorm(x, norm_gain[0], mod_rows, lay)
    for layer in range(depth):
        z, qk1, qk2, v1, v2 = _in_proj(h, w_in, layer, cos, sin, lay)
        o0, l0 = _band_attention(z[None], z[None], ZC_Q0 // gw, ZC_K0 // gw, ZC_V0 // gw, DILATIONS[0], lay)
        attn = [
            (o0[0], l0[0]),
            _band_attention(qk1, v1, 0, 1, 0, DILATIONS[1], lay),
            _band_attention(qk2, v2, 0, 1, 0, DILATIONS[2], lay),
        ]
        if layer + 1 < depth:
            x, h = _merge(z, attn, x, mod_rows, layer, *weights, gains, layer + 1, lay)
        else:
            last = (z, attn, x, mod_rows, layer, *weights, final_gain.reshape(1, 1, d), 0, lay)
            y_prompt = _merge(*last, final=True, row_start=0, rows=bp * sp)
            y_sample = _merge(*last, final=True, row_start=bp * sp, rows=bs * ss)
    return (y_prompt.reshape(bp, sp, d), y_sample.reshape(bs, ss, d))
```
